```python
import jax, jax.numpy as jnp
from jax import lax
import numpy as np

D_MODEL = 2048
BATCH = 8
SEQ = 2048
DEPTH = 1

CHUNK = 64
EPS = 1e-6
CONV_WIDTH = D_MODEL // 2
CONV_K = 3
GLA_HEADS = 4
GLA_DK = D_MODEL // 16
GLA_DV = D_MODEL // 8
GLA_LOWRANK = 16
GLA_TAU = 16.0
N_EXPERTS = 32
TOP_K = 4
D_FF = D_MODEL
SWIGLU_LIMIT = 7.0
SWIGLU_ALPHA = 1.702
EXPERT_BLOCK = 128

SPLIT_SIZES = (
    CONV_WIDTH,
    CONV_WIDTH,
    CONV_WIDTH,
    GLA_HEADS * GLA_DK,
    GLA_HEADS * GLA_DK,
    GLA_HEADS * GLA_DV,
    GLA_HEADS * GLA_DV,
    GLA_LOWRANK,
    D_MODEL,
    D_MODEL,
)
SPLIT_IDX = tuple(int(i) for i in np.cumsum(SPLIT_SIZES)[:-1])
D_IN_PROJ = int(sum(SPLIT_SIZES))

kernel_name = "hybrid_conv_gla_moe_block"


def rms_norm(x, g):
    xf = x.astype(jnp.float32)
    y = xf * lax.rsqrt(jnp.mean(xf * xf, axis=-1, keepdims=True) + EPS)
    return (y * g.astype(jnp.float32)).astype(x.dtype)


def short_gated_conv(u_c, u_b, u_x, conv_w, conv_b):
    y = u_c * u_x
    t = y.shape[1]
    yp = jnp.pad(y, ((0, 0), (CONV_K - 1, 0), (0, 0)))
    z = conv_b + sum(yp[:, j:j + t] * conv_w[j] for j in range(CONV_K))
    return u_b * z


def chunk_causal_gla(q, k, v, r, a_low, w_a2, b_a, norm_g):
    out_dtype = q.dtype
    bn, t = q.shape[:2]
    nc = t // CHUNK
    f32 = jnp.float32
    shp_k = (bn, nc, CHUNK, GLA_HEADS, GLA_DK)
    shp_v = (bn, nc, CHUNK, GLA_HEADS, GLA_DV)
    q = q.astype(f32).reshape(shp_k) * (GLA_DK ** -0.5)
    k = k.astype(f32).reshape(shp_k)
    v = v.astype(f32).reshape(shp_v)
    log_a = jax.nn.log_sigmoid((a_low @ w_a2 + b_a).astype(f32)) / GLA_TAU
    log_a = log_a.reshape(shp_k)
    bcum = jnp.cumsum(log_a, axis=2)
    a_end = bcum[:, :, -1]
    k_t = k * jnp.exp(a_end[:, :, None] - bcum)
    q_i = q * jnp.exp(a_end)[:, :, None]
    scores = jnp.einsum('bclhk,bcshk->bchls', q, k_t)
    o_intra = jnp.einsum('bchls,bcshv->bclhv', scores, v)

    def step(state, inp):
        qc, kc, vc, dc = inp
        o = jnp.einsum('blhk,bhkv->blhv', qc, state)
        state = jnp.exp(dc)[..., None] * state + jnp.einsum('blhk,blhv->bhkv', kc, vc)
        return state, o

    xs = (jnp.moveaxis(q_i, 1, 0), jnp.moveaxis(k_t, 1, 0),
          jnp.moveaxis(v, 1, 0), jnp.moveaxis(a_end, 1, 0))
    s0 = jnp.zeros((bn, GLA_HEADS, GLA_DK, GLA_DV), f32)
    _, o_inter = lax.scan(step, s0, xs)
    o = (o_intra + jnp.moveaxis(o_inter, 0, 1)).reshape(bn, t, GLA_HEADS, GLA_DV)
    o = o * lax.rsqrt(jnp.mean(o * o, axis=-1, keepdims=True) + EPS) * norm_g.astype(f32)
    o = o * jax.nn.silu(r.astype(f32)).reshape(bn, t, GLA_HEADS, GLA_DV)
    return o.reshape(bn, t, GLA_HEADS * GLA_DV).astype(out_dtype)


def moe_ffn(xt, w_router, b_router, w_gate_up, b_gate_up, w_down, b_down):
    n, d = xt.shape
    nk = n * TOP_K
    logits = (xt @ w_router + b_router).astype(jnp.float32)
    top_v, top_i = lax.top_k(logits, TOP_K)
    top_w = jax.nn.softmax(top_v, axis=-1)
    e_flat = top_i.reshape(-1)
    tok_flat = jnp.arange(nk, dtype=jnp.int32) // TOP_K
    w_flat = top_w.reshape(-1)

    order = jnp.argsort(e_flat)
    sorted_e = e_flat[order]
    counts = jnp.bincount(e_flat, length=N_EXPERTS)
    start = jnp.cumsum(counts) - counts
    padded = ((counts + EXPERT_BLOCK - 1) // EXPERT_BLOCK) * EXPERT_BLOCK
    pad_end = jnp.cumsum(padded)
    pad_start = pad_end - padded
    rank = jnp.arange(nk, dtype=jnp.int32) - start[sorted_e]
    dest = pad_start[sorted_e] + rank

    n_slots = nk + N_EXPERTS * EXPERT_BLOCK
    n_blocks = n_slots // EXPERT_BLOCK
    slot_tok = jnp.full((n_slots,), n, dtype=jnp.int32).at[dest].set(tok_flat[order])
    slot_w = jnp.zeros((n_slots,), jnp.float32).at[dest].set(w_flat[order])
    block_start = jnp.arange(n_blocks, dtype=jnp.int32) * EXPERT_BLOCK
    block_e = jnp.minimum(jnp.searchsorted(pad_end, block_start, side='right'), N_EXPERTS - 1)

    x_pad = jnp.concatenate([xt, jnp.zeros((1, d), xt.dtype)], axis=0)
    xb = x_pad[slot_tok].reshape(n_blocks, EXPERT_BLOCK, d)

    def expert_block(args):
        xblk, e = args
        gu = xblk @ w_gate_up[e] + b_gate_up[e]
        gate = jnp.minimum(gu[:, :D_FF], SWIGLU_LIMIT)
        up = jnp.clip(gu[:, D_FF:], -SWIGLU_LIMIT, SWIGLU_LIMIT)
        glu = gate * jax.nn.sigmoid(SWIGLU_ALPHA * gate)
        return ((up + 1.0) * glu) @ w_down[e] + b_down[e]

    yb = lax.map(expert_block, (xb, block_e))
    y_slots = yb.reshape(n_slots, d) * slot_w[:, None].astype(yb.dtype)
    return jax.ops.segment_sum(y_slots, slot_tok, num_segments=n + 1)[:n]


def setup_inputs(seed: int = 0) -> dict:
    key = jax.random.key(seed)
    ks = jax.random.split(key, 20)

    def nrm(k, shape, scale):
        return jax.random.normal(k, shape, jnp.float32) * scale

    L = DEPTH
    return {
        "x": nrm(ks[0], (BATCH, SEQ, D_MODEL), 1.0),
        "norm_mix_g": 1.0 + nrm(ks[1], (L, D_MODEL), 0.02),
        "w_in": nrm(ks[2], (L, D_MODEL, D_IN_PROJ), D_MODEL ** -0.5),
        "conv_w": nrm(ks[3], (L, CONV_K, CONV_WIDTH), CONV_K ** -0.5),
        "conv_b": nrm(ks[4], (L, CONV_WIDTH), 0.01),
        "w_a2": nrm(ks[5], (L, GLA_LOWRANK, GLA_HEADS * GLA_DK), GLA_LOWRANK ** -0.5),
        "b_a": nrm(ks[6], (L, GLA_HEADS * GLA_DK), 0.1),
        "gla_norm_g": 1.0 + nrm(ks[7], (L, GLA_HEADS, GLA_DV), 0.02),
        "w_br_a": nrm(ks[8], (L, CONV_WIDTH, D_MODEL), CONV_WIDTH ** -0.5),
        "w_br_b": nrm(ks[9], (L, GLA_HEADS * GLA_DV, D_MODEL), (GLA_HEADS * GLA_DV) ** -0.5),
        "w_out": nrm(ks[10], (L, D_MODEL, D_MODEL), D_MODEL ** -0.5),
        "norm_ffn_g": 1.0 + nrm(ks[11], (L, D_MODEL), 0.02),
        "w_router": nrm(ks[12], (L, D_MODEL, N_EXPERTS), D_MODEL ** -0.5),
        "b_router": nrm(ks[13], (L, N_EXPERTS), 0.01),
        "w_gate_up": nrm(ks[14], (L, N_EXPERTS, D_MODEL, 2 * D_FF), D_MODEL ** -0.5),
        "b_gate_up": nrm(ks[15], (L, N_EXPERTS, 2 * D_FF), 0.01),
        "w_down": nrm(ks[16], (L, N_EXPERTS, D_FF, D_MODEL), D_FF ** -0.5),
        "b_down": nrm(ks[17], (L, N_EXPERTS, D_MODEL), 0.01),
        "norm_final_g": 1.0 + nrm(ks[18], (D_MODEL,), 0.02),
    }


def reference(x, norm_mix_g, w_in, conv_w, conv_b, w_a2, b_a, gla_norm_g, w_br_a, w_br_b,
              w_out, norm_ffn_g, w_router, b_router, w_gate_up, b_gate_up, w_down, b_down,
              norm_final_g):
    bn, t, d = x.shape
    for l in range(DEPTH):
        h = rms_norm(x, norm_mix_g[l])
        proj = h @ w_in[l]
        u_c, u_b, u_x, q, k, v, r, a_low, g_a, g_b = jnp.split(proj, SPLIT_IDX, axis=-1)
        y_a = short_gated_conv(u_c, u_b, u_x, conv_w[l], conv_b[l]) @ w_br_a[l]
        y_b = chunk_causal_gla(q, k, v, r, a_low, w_a2[l], b_a[l], gla_norm_g[l]) @ w_br_b[l]
        mixed = jax.nn.sigmoid(g_a) * y_a + jax.nn.sigmoid(g_b) * y_b
        x = x + mixed @ w_out[l]
        h = rms_norm(x, norm_ffn_g[l]).reshape(bn * t, d)
        y = moe_ffn(h, w_router[l], b_router[l], w_gate_up[l], b_gate_up[l], w_down[l], b_down[l])
        x = x + y.reshape(bn, t, d)
    return rms_norm(x, norm_final_g)
```

```python
import functools

import jax
import jax.numpy as jnp
from jax import lax
from jax.experimental import pallas as pl
from jax.experimental.pallas import tpu as pltpu

F32 = jnp.float32
BF16 = jnp.bfloat16
U32 = jnp.uint32

D_MODEL = 2048
CHUNK = 64
EPS = 1e-6
CONV_WIDTH = 1024
CONV_K = 3
GLA_HEADS = 4
GLA_DK = 128
GLA_DV = 256
GLA_LOWRANK = 16
GLA_TAU = 16.0
N_EXPERTS = 32
TOP_K = 4
D_FF = 2048
SWIGLU_LIMIT = 7.0
SWIGLU_ALPHA = 1.702

LANES = 128
HALO = 8
HALF = D_MODEL // 2

COL_UC, COL_UB, COL_UX = 0, 1024, 2048
COL_Q, COL_K, COL_V, COL_R = 3072, 3584, 4096, 5120
COL_GA, COL_GB = 6144, 8192
N_PROJ = 10240
A_LOW_START = 6144

MIB = 1024 * 1024


def _rms(x, g):
    return x * lax.rsqrt(jnp.mean(x * x, axis=-1, keepdims=True) + EPS) * g


def _pack_bf16_pair(x):
    xr = x.astype(BF16).astype(F32)
    hi = lax.bitcast_convert_type(xr[:, :HALF], U32) & jnp.uint32(0xFFFF0000)
    lo = lax.bitcast_convert_type(xr[:, HALF:], U32) >> jnp.uint32(16)
    return hi | lo


def _unpack_hi(p):
    return lax.bitcast_convert_type(p & jnp.uint32(0xFFFF0000), F32)


def _unpack_lo(p):
    return lax.bitcast_convert_type(p << jnp.uint32(16), F32)


def _inproj_kernel(x_ref, g_ref, w_ref, wa_ref, p_ref, a_ref, h_ref):
    @pl.when(pl.program_id(1) == 0)
    def _():
        hb = _rms(x_ref[...], g_ref[...]).astype(BF16)
        h_ref[...] = hb
        a_ref[...] = jnp.dot(hb, wa_ref[...], preferred_element_type=F32)

    p_ref[...] = jnp.dot(h_ref[...], w_ref[...], preferred_element_type=F32).astype(BF16)


def _inproj(x2d, g, w_cat, w_alow, *, tm=1024, tn=1024):
    n = x2d.shape[0]
    return pl.pallas_call(
        _inproj_kernel,
        grid=(n // tm, N_PROJ // tn),
        in_specs=[
            pl.BlockSpec((tm, D_MODEL), lambda m, j: (m, 0)),
            pl.BlockSpec((1, D_MODEL), lambda m, j: (0, 0)),
            pl.BlockSpec((D_MODEL, tn), lambda m, j: (0, j)),
            pl.BlockSpec((D_MODEL, LANES), lambda m, j: (0, 0)),
        ],
        out_specs=[
            pl.BlockSpec((tm, tn), lambda m, j: (m, j)),
            pl.BlockSpec((tm, LANES), lambda m, j: (m, 0)),
        ],
        out_shape=[
            jax.ShapeDtypeStruct((n, N_PROJ), BF16),
            jax.ShapeDtypeStruct((n, LANES), F32),
        ],
        scratch_shapes=[pltpu.VMEM((tm, D_MODEL), BF16)],
        compiler_params=pltpu.CompilerParams(
            dimension_semantics=("arbitrary", "arbitrary"), vmem_limit_bytes=48 * MIB),
        name="inproj",
    )(x2d, g, w_cat, w_alow)


def _mixer_kernel(uc_ref, ub_ref, ux_ref, q_ref, k_ref, v_ref, r_ref, a_ref,
                  cw_ref, cb_ref, wa2_ref, ba_ref, ng_ref,
                  ya_ref, yb_ref, st_ref, ybuf_ref, *, tt):
    t = pl.program_id(1)

    @pl.when(t == 0)
    def _():
        st_ref[...] = jnp.zeros_like(st_ref)
        ybuf_ref[0:HALO, :] = jnp.zeros((HALO, CONV_WIDTH), F32)

    @pl.when(t > 0)
    def _():
        ybuf_ref[0:HALO, :] = ybuf_ref[tt:tt + HALO, :]

    y = uc_ref[...].astype(F32) * ux_ref[...].astype(F32)
    ybuf_ref[HALO:HALO + tt, :] = y
    y1 = ybuf_ref[HALO - 1:HALO - 1 + tt, :]
    y2 = ybuf_ref[HALO - 2:HALO - 2 + tt, :]
    z = cb_ref[...] + cw_ref[0:1, :] * y2 + cw_ref[1:2, :] * y1 + cw_ref[2:3, :] * y
    ya_ref[...] = (ub_ref[...].astype(F32) * z).astype(BF16)

    row = lax.broadcasted_iota(jnp.int32, (CHUNK, CHUNK), 0)
    col = lax.broadcasted_iota(jnp.int32, (CHUNK, CHUNK), 1)
    tri = (row >= col).astype(F32)
    scale = GLA_DK ** -0.5
    for c in range(tt // CHUNK):
        rows = slice(c * CHUNK, (c + 1) * CHUNK)
        zf = jnp.dot(a_ref[rows, :], wa2_ref[...], preferred_element_type=F32,
                     precision=lax.Precision.HIGHEST) + ba_ref[...]
        log_a = (jnp.minimum(zf, 0.0) - jnp.log(1.0 + jnp.exp(-jnp.abs(zf)))) * (1.0 / GLA_TAU)
        bcum = jnp.dot(tri, log_a, preferred_element_type=F32, precision=lax.Precision.HIGHEST)
        a_end = bcum[CHUNK - 1:CHUNK, :]
        k_t = (k_ref[rows, :].astype(F32) * jnp.exp(a_end - bcum)).astype(BF16)
        decay = jnp.exp(a_end)
        for h in range(GLA_HEADS):
            ks = slice(h * GLA_DK, (h + 1) * GLA_DK)
            vs = slice(h * GLA_DV, (h + 1) * GLA_DV)
            kv_t = lax.dot_general(v_ref[rows, vs], k_t[:, ks], (((0,), (0,)), ((), ())),
                                   preferred_element_type=F32)
            s_t = decay[:, ks] * st_ref[h] + kv_t
            st_ref[h] = s_t
            o = lax.dot_general(q_ref[rows, ks], s_t.astype(BF16), (((1,), (1,)), ((), ())),
                                preferred_element_type=F32) * scale
            o = o * lax.rsqrt(jnp.mean(o * o, axis=-1, keepdims=True) + EPS) * ng_ref[:, vs]
            yb_ref[rows, vs] = (o * jax.nn.silu(r_ref[rows, vs].astype(F32))).astype(BF16)


def _mixer(proj, a_low, conv_w, conv_b, w_a2p, b_a, norm_g, *, batch, seq, tt=256):
    n = proj.shape[0]
    nt = seq // tt

    def col(width, start):
        return pl.BlockSpec((tt, width), lambda b, t: (b * nt + t, start // width))

    def whole(shape):
        return pl.BlockSpec(shape, lambda b, t: (0,) * len(shape))

    return pl.pallas_call(
        functools.partial(_mixer_kernel, tt=tt),
        grid=(batch, nt),
        in_specs=[
            col(CONV_WIDTH, COL_UC), col(CONV_WIDTH, COL_UB), col(CONV_WIDTH, COL_UX),
            col(GLA_HEADS * GLA_DK, COL_Q), col(GLA_HEADS * GLA_DK, COL_K),
            col(GLA_HEADS * GLA_DV, COL_V), col(GLA_HEADS * GLA_DV, COL_R),
            pl.BlockSpec((tt, LANES), lambda b, t: (b * nt + t, 0)),
            whole((CONV_K, CONV_WIDTH)), whole((1, CONV_WIDTH)),
            whole((LANES, GLA_HEADS * GLA_DK)), whole((1, GLA_HEADS * GLA_DK)),
            whole((1, GLA_HEADS * GLA_DV)),
        ],
        out_specs=[
            pl.BlockSpec((tt, CONV_WIDTH), lambda b, t: (b * nt + t, 0)),
            pl.BlockSpec((tt, GLA_HEADS * GLA_DV), lambda b, t: (b * nt + t, 0)),
        ],
        out_shape=[
            jax.ShapeDtypeStruct((n, CONV_WIDTH), BF16),
            jax.ShapeDtypeStruct((n, GLA_HEADS * GLA_DV), BF16),
        ],
        scratch_shapes=[
            pltpu.VMEM((GLA_HEADS, GLA_DV, GLA_DK), F32),
            pltpu.VMEM((tt + HALO, CONV_WIDTH), F32),
        ],
        compiler_params=pltpu.CompilerParams(
            dimension_semantics=("arbitrary", "arbitrary"), vmem_limit_bytes=32 * MIB),
        name="mixer",
    )(proj, proj, proj, proj, proj, proj, proj, a_low, conv_w, conv_b, w_a2p, b_a, norm_g)


def _proj_kernel(ya_ref, yb_ref, ga_ref, gb_ref, x_ref, wa_ref, wb_ref, wo_ref, ng_ref, wr_ref, br_ref,
                 x1_ref, hp_ref, ti_ref, tw_ref):
    y_a = jnp.dot(ya_ref[...], wa_ref[...], preferred_element_type=F32)
    y_b = jnp.dot(yb_ref[...], wb_ref[...], preferred_element_type=F32)
    mixed = (jax.nn.sigmoid(ga_ref[...].astype(F32)) * y_a
             + jax.nn.sigmoid(gb_ref[...].astype(F32)) * y_b)
    x1 = x_ref[...] + jnp.dot(mixed.astype(BF16), wo_ref[...], preferred_element_type=F32)
    x1_ref[...] = x1
    h2 = _rms(x1, ng_ref[...])
    hp_ref[...] = _pack_bf16_pair(h2)

    logits = jnp.dot(h2, wr_ref[...], preferred_element_type=F32,
                     precision=lax.Precision.HIGHEST) + br_ref[...]
    lane = lax.broadcasted_iota(jnp.int32, logits.shape, 1)
    neg = jnp.float32(-jnp.inf)
    work = jnp.where(lane < N_EXPERTS, logits, neg)
    top_i = jnp.zeros(logits.shape, jnp.int32)
    top_e = jnp.zeros(logits.shape, F32)
    v0 = None
    for kk in range(TOP_K):
        m = jnp.max(work, axis=-1, keepdims=True)
        idx = jnp.min(jnp.where(work == m, lane, LANES), axis=-1, keepdims=True)
        if kk == 0:
            v0 = m
        top_i = jnp.where(lane == kk, idx, top_i)
        top_e = jnp.where(lane == kk, jnp.exp(m - v0), top_e)
        work = jnp.where(lane == idx, neg, work)
    ti_ref[...] = top_i
    tw_ref[...] = top_e / jnp.sum(top_e, axis=-1, keepdims=True)


def _proj(ya_in, yb_in, proj, x2d, w_br_a, w_br_b, w_out, norm_g, w_router, b_router, *, tm=256):
    n = x2d.shape[0]

    def rows(width, cblk=0):
        return pl.BlockSpec((tm, width), lambda i: (i, cblk))

    def whole(shape):
        return pl.BlockSpec(shape, lambda i: (0,) * len(shape), pipeline_mode=pl.Buffered(1))

    return pl.pallas_call(
        _proj_kernel,
        grid=(n // tm,),
        in_specs=[
            rows(CONV_WIDTH), rows(GLA_HEADS * GLA_DV),
            rows(D_MODEL, COL_GA // D_MODEL), rows(D_MODEL, COL_GB // D_MODEL),
            rows(D_MODEL),
            whole((CONV_WIDTH, D_MODEL)), whole((GLA_HEADS * GLA_DV, D_MODEL)), whole((D_MODEL, D_MODEL)),
            whole((1, D_MODEL)), whole((D_MODEL, LANES)), whole((1, LANES)),
        ],
        out_specs=[rows(D_MODEL), rows(HALF), rows(LANES), rows(LANES)],
        out_shape=[
            jax.ShapeDtypeStruct((n, D_MODEL), F32),
            jax.ShapeDtypeStruct((n, HALF), U32),
            jax.ShapeDtypeStruct((n, LANES), jnp.int32),
            jax.ShapeDtypeStruct((n, LANES), F32),
        ],
        compiler_params=pltpu.CompilerParams(
            dimension_semantics=("arbitrary",), vmem_limit_bytes=56 * MIB),
        name="proj",
    )(ya_in, yb_in, proj, proj, x2d, w_br_a, w_br_b, w_out, norm_g, w_router, b_router)


def _gather_kernel(nrows_ref, tok_ref, src_ref, dst_ref, zero_ref, sem, *, g):
    i = pl.program_id(0)
    base = i * g

    @pl.when(i == 0)
    def _():
        zero_ref[...] = jnp.zeros_like(zero_ref)

    @pl.when(base >= nrows_ref[0])
    def _():
        fill = pltpu.make_async_copy(zero_ref, dst_ref.at[pl.ds(base, g)], sem)
        fill.start()
        fill.wait()

    @pl.when(base < nrows_ref[0])
    def _():
        def issue(j, carry):
            pltpu.make_async_copy(src_ref.at[pl.ds(tok_ref[j], 1)],
                                  dst_ref.at[pl.ds(base + j, 1)], sem).start()
            return carry

        lax.fori_loop(0, g, issue, 0, unroll=8)
        pltpu.make_async_copy(src_ref.at[pl.ds(0, g)], dst_ref.at[pl.ds(base, g)], sem).wait()


def _gather(n_rows_used, slot_tok, hp, *, n_slots, g=1024):
    return pl.pallas_call(
        functools.partial(_gather_kernel, g=g),
        grid_spec=pltpu.PrefetchScalarGridSpec(
            num_scalar_prefetch=1,
            grid=(n_slots // g,),
            in_specs=[
                pl.BlockSpec((g,), lambda i, nr: (i,), memory_space=pltpu.SMEM),
                pl.BlockSpec(memory_space=pl.ANY),
            ],
            out_specs=pl.BlockSpec(memory_space=pl.ANY),
            scratch_shapes=[pltpu.VMEM((g, HALF), U32), pltpu.SemaphoreType.DMA(())],
        ),
        out_shape=jax.ShapeDtypeStruct((n_slots, HALF), U32),
        compiler_params=pltpu.CompilerParams(dimension_semantics=("arbitrary",)),
        name="gather",
    )(n_rows_used, slot_tok, hp)


def _experts_kernel(te_ref, nu_ref, xp_ref, wg_ref, wu_ref, wd_ref, bg_ref, bu_ref, bd_ref,
                    out_ref, xb_ref, acc_ref, *, n_f):
    i = pl.program_id(0)
    f = pl.program_id(1)

    @pl.when((i >= nu_ref[0]) & (f == n_f - 1))
    def _():
        out_ref[...] = jnp.zeros_like(out_ref)

    @pl.when(i < nu_ref[0])
    def _():
        @pl.when(f == 0)
        def _():
            p = xp_ref[...]
            xb_ref[:, :HALF] = _unpack_hi(p).astype(BF16)
            xb_ref[:, HALF:] = _unpack_lo(p).astype(BF16)

        x = xb_ref[...]
        gate = jnp.dot(x, wg_ref[...].astype(BF16), preferred_element_type=F32) + bg_ref[...]
        up = jnp.dot(x, wu_ref[...].astype(BF16), preferred_element_type=F32) + bu_ref[...]
        gate = jnp.minimum(gate, SWIGLU_LIMIT)
        up = jnp.clip(up, -SWIGLU_LIMIT, SWIGLU_LIMIT)
        act = (up + 1.0) * (gate * jax.nn.sigmoid(SWIGLU_ALPHA * gate))
        contrib = jnp.dot(act.astype(BF16), wd_ref[...].astype(BF16), preferred_element_type=F32)

        @pl.when(f == 0)
        def _():
            acc_ref[...] = contrib + bd_ref[...]

        @pl.when(f > 0)
        def _():
            acc_ref[...] += contrib

        @pl.when(f == n_f - 1)
        def _():
            out_ref[...] = _pack_bf16_pair(acc_ref[...])


def _experts(tile_e, n_used, xs, w_gate_up, w_down, b_gate_up, b_down, *, tm, tf=256):
    n_slots = xs.shape[0]
    n_f = D_FF // tf

    def tile(i, nu):
        return jnp.minimum(i, nu[0] - 1)

    def fblk(i, f, nu):
        return jnp.where(i < nu[0], f, n_f - 1)

    return pl.pallas_call(
        functools.partial(_experts_kernel, n_f=n_f),
        grid_spec=pltpu.PrefetchScalarGridSpec(
            num_scalar_prefetch=2,
            grid=(n_slots // tm, n_f),
            in_specs=[
                pl.BlockSpec((tm, HALF), lambda i, f, te, nu: (tile(i, nu), 0)),
                pl.BlockSpec((None, D_MODEL, tf), lambda i, f, te, nu: (te[tile(i, nu)], 0, fblk(i, f, nu))),
                pl.BlockSpec((None, D_MODEL, tf),
                             lambda i, f, te, nu: (te[tile(i, nu)], 0, n_f + fblk(i, f, nu))),
                pl.BlockSpec((None, tf, D_MODEL), lambda i, f, te, nu: (te[tile(i, nu)], fblk(i, f, nu), 0)),
                pl.BlockSpec((None, 1, tf), lambda i, f, te, nu: (te[tile(i, nu)], 0, fblk(i, f, nu))),
                pl.BlockSpec((None, 1, tf), lambda i, f, te, nu: (te[tile(i, nu)], 0, n_f + fblk(i, f, nu))),
                pl.BlockSpec((None, 1, D_MODEL), lambda i, f, te, nu: (te[tile(i, nu)], 0, 0)),
            ],
            out_specs=pl.BlockSpec((tm, HALF), lambda i, f, te, nu: (i, 0)),
            scratch_shapes=[pltpu.VMEM((tm, D_MODEL), BF16), pltpu.VMEM((tm, D_MODEL), F32)],
        ),
        out_shape=jax.ShapeDtypeStruct((n_slots, HALF), U32),
        compiler_params=pltpu.CompilerParams(
            dimension_semantics=("arbitrary", "arbitrary"), vmem_limit_bytes=48 * MIB),
        name="experts",
    )(tile_e, n_used, xs, w_gate_up, w_gate_up, w_down, b_gate_up, b_gate_up, b_down)


def _combine_kernel(dest_ref, dnext_ref, tw_ref, x1_ref, g_ref, yb_ref, out_ref, buf_ref, sem, *, tc, n_tiles):
    i = pl.program_id(0)
    slot = i % 2

    def issue(idx_ref, s):
        def body(j, carry):
            pltpu.make_async_copy(yb_ref.at[pl.ds(idx_ref[j], 1)],
                                  buf_ref.at[s, pl.ds(j, 1)], sem.at[s]).start()
            return carry
        lax.fori_loop(0, TOP_K * tc, body, 0, unroll=8)

    @pl.when(i == 0)
    def _():
        issue(dest_ref, 0)

    @pl.when(i + 1 < n_tiles)
    def _():
        issue(dnext_ref, 1 - slot)

    pltpu.make_async_copy(yb_ref.at[pl.ds(0, TOP_K * tc)], buf_ref.at[slot], sem.at[slot]).wait()

    lane = lax.broadcasted_iota(jnp.int32, (tc, LANES), 1)
    tw = tw_ref[...]
    acc_hi = jnp.zeros((tc, HALF), F32)
    acc_lo = jnp.zeros((tc, HALF), F32)
    for kk in range(TOP_K):
        w = jnp.sum(jnp.where(lane == kk, tw, 0.0), axis=-1, keepdims=True)
        p = buf_ref[slot, kk * tc:(kk + 1) * tc, :]
        acc_hi = acc_hi + w * _unpack_hi(p)
        acc_lo = acc_lo + w * _unpack_lo(p)
    x2_hi = x1_ref[:, :HALF] + acc_hi
    x2_lo = x1_ref[:, HALF:] + acc_lo
    ms = (jnp.sum(x2_hi * x2_hi, axis=-1, keepdims=True)
          + jnp.sum(x2_lo * x2_lo, axis=-1, keepdims=True)) * (1.0 / D_MODEL)
    inv = lax.rsqrt(ms + EPS)
    out_ref[:, :HALF] = x2_hi * inv * g_ref[:, :HALF]
    out_ref[:, HALF:] = x2_lo * inv * g_ref[:, HALF:]


def _combine(dest_kt, top_w, x1, norm_g, yb, *, tc=256):
    n = x1.shape[0]
    n_tiles = n // tc
    return pl.pallas_call(
        functools.partial(_combine_kernel, tc=tc, n_tiles=n_tiles),
        grid=(n_tiles,),
        in_specs=[
            pl.BlockSpec((TOP_K * tc,), lambda i: (i,), memory_space=pltpu.SMEM),
            pl.BlockSpec((TOP_K * tc,), lambda i: (jnp.minimum(i + 1, n_tiles - 1),), memory_space=pltpu.SMEM),
            pl.BlockSpec((tc, LANES), lambda i: (i, 0)),
            pl.BlockSpec((tc, D_MODEL), lambda i: (i, 0)),
            pl.BlockSpec((1, D_MODEL), lambda i: (0, 0)),
            pl.BlockSpec(memory_space=pl.ANY),
        ],
        out_specs=pl.BlockSpec((tc, D_MODEL), lambda i: (i, 0)),
        out_shape=jax.ShapeDtypeStruct((n, D_MODEL), F32),
        scratch_shapes=[pltpu.VMEM((2, TOP_K * tc, HALF), U32), pltpu.SemaphoreType.DMA((2,))],
        compiler_params=pltpu.CompilerParams(
            dimension_semantics=("arbitrary",), vmem_limit_bytes=40 * MIB),
        name="combine",
    )(dest_kt, dest_kt, top_w, x1, norm_g, yb)


def _route(top_i, *, tm):
    n = top_i.shape[0]
    nk = n * TOP_K
    e_flat = top_i.reshape(-1)
    order = jnp.argsort(e_flat).astype(jnp.int32)
    sorted_e = e_flat[order]
    counts = jnp.bincount(e_flat, length=N_EXPERTS).astype(jnp.int32)
    start = jnp.cumsum(counts) - counts
    padded = ((counts + tm - 1) // tm) * tm
    pad_end = jnp.cumsum(padded)
    pad_start = pad_end - padded
    rank = jnp.arange(nk, dtype=jnp.int32) - start[sorted_e]
    dest_sorted = (pad_start[sorted_e] + rank).astype(jnp.int32)
    n_slots = nk + N_EXPERTS * tm
    slot_tok = jnp.zeros((n_slots,), jnp.int32).at[dest_sorted].set(order // TOP_K)
    dest_flat = jnp.zeros((nk,), jnp.int32).at[order].set(dest_sorted)
    tile_start = jnp.arange(n_slots // tm, dtype=jnp.int32) * tm
    tile_e = jnp.minimum(jnp.searchsorted(pad_end, tile_start, side="right"), N_EXPERTS - 1).astype(jnp.int32)
    n_rows_used = pad_end[-1:].astype(jnp.int32)
    return slot_tok, dest_flat.reshape(n, TOP_K), tile_e, n_rows_used, n_slots


def kernel(x, norm_mix_g, w_in, conv_w, conv_b, w_a2, b_a, gla_norm_g, w_br_a, w_br_b, w_out, norm_ffn_g,
           w_router, b_router, w_gate_up, b_gate_up, w_down, b_down, norm_final_g):
    bn, t, d = x.shape
    n = bn * t
    tm_e = 512
    tc = 256
    x2d = x.reshape(n, d)
    assert w_in.shape[0] == 1, "single-layer block: the final norm is fused into the combine step"
    for l in range(1):
        wl = w_in[l]
        w_cat = jnp.concatenate([wl[:, :A_LOW_START], wl[:, A_LOW_START + GLA_LOWRANK:]], axis=1).astype(BF16)
        w_alow = jnp.pad(wl[:, A_LOW_START:A_LOW_START + GLA_LOWRANK],
                         ((0, 0), (0, LANES - GLA_LOWRANK))).astype(BF16)
        proj, a_low = _inproj(x2d, norm_mix_g[l][None, :], w_cat, w_alow)

        w_a2p = jnp.pad(w_a2[l], ((0, LANES - GLA_LOWRANK), (0, 0)))
        ya_in, yb_in = _mixer(proj, a_low, conv_w[l], conv_b[l][None, :], w_a2p, b_a[l][None, :],
                              gla_norm_g[l].reshape(1, -1), batch=bn, seq=t)

        w_rp = jnp.pad(w_router[l], ((0, 0), (0, LANES - N_EXPERTS)))
        b_rp = jnp.pad(b_router[l], (0, LANES - N_EXPERTS))[None, :]
        x1, hp, top_i, top_w = _proj(ya_in, yb_in, proj, x2d, w_br_a[l].astype(BF16), w_br_b[l].astype(BF16),
                                     w_out[l].astype(BF16), norm_ffn_g[l][None, :], w_rp, b_rp)

        slot_tok, dest, tile_e, n_rows_used, n_slots = _route(top_i[:, :TOP_K], tm=tm_e)
        xs = _gather(n_rows_used, slot_tok, hp, n_slots=n_slots)
        yb = _experts(tile_e, n_rows_used // tm_e, xs, w_gate_up[l], w_down[l],
                      b_gate_up[l][:, None, :], b_down[l][:, None, :], tm=tm_e)

        dest_kt = dest.reshape(n // tc, tc, TOP_K).transpose(0, 2, 1).reshape(-1)
        x2d = _combine(dest_kt, top_w, x1, norm_final_g[None, :], yb, tc=tc)
    return x2d.reshape(bn, t, d)
```

```python
import functools

import jax
import jax.numpy as jnp
from jax import lax
from jax.experimental import pallas as pl
from jax.experimental.pallas import tpu as pltpu

F32 = jnp.float32
BF16 = jnp.bfloat16

D_MODEL = 2048
CHUNK = 64
EPS = 1e-6
CONV_WIDTH = 1024
CONV_K = 3
GLA_HEADS = 4
GLA_DK = 128
GLA_DV = 256
GLA_LOWRANK = 16
GLA_TAU = 16.0
N_EXPERTS = 32
TOP_K = 4
D_FF = 2048
SWIGLU_LIMIT = 7.0
SWIGLU_ALPHA = 1.702

LANES = 128
HALO = 8

COL_UC, COL_UB, COL_UX = 0, 1024, 2048
COL_Q, COL_K, COL_V, COL_R = 3072, 3584, 4096, 5120
COL_GA, COL_GB = 6144, 8192
N_PROJ = 10240
A_LOW_START = 6144

MIB = 1024 * 1024


def _rms(x, g):
    return x * lax.rsqrt(jnp.mean(x * x, axis=-1, keepdims=True) + EPS) * g


def _inproj_kernel(x_ref, g_ref, w_ref, wa_ref, p_ref, a_ref, h_ref):
    @pl.when(pl.program_id(1) == 0)
    def _():
        hb = _rms(x_ref[...], g_ref[...]).astype(BF16)
        h_ref[...] = hb
        a_ref[...] = jnp.dot(hb, wa_ref[...], preferred_element_type=F32)

    p_ref[...] = jnp.dot(h_ref[...], w_ref[...], preferred_element_type=F32).astype(BF16)


def _inproj(x2d, g, w_cat, w_alow, *, tm=1024, tn=1024):
    n = x2d.shape[0]
    return pl.pallas_call(
        _inproj_kernel,
        grid=(n // tm, N_PROJ // tn),
        in_specs=[
            pl.BlockSpec((tm, D_MODEL), lambda m, j: (m, 0)),
            pl.BlockSpec((1, D_MODEL), lambda m, j: (0, 0)),
            pl.BlockSpec((D_MODEL, tn), lambda m, j: (0, j)),
            pl.BlockSpec((D_MODEL, LANES), lambda m, j: (0, 0)),
        ],
        out_specs=[
            pl.BlockSpec((tm, tn), lambda m, j: (m, j)),
            pl.BlockSpec((tm, LANES), lambda m, j: (m, 0)),
        ],
        out_shape=[
            jax.ShapeDtypeStruct((n, N_PROJ), BF16),
            jax.ShapeDtypeStruct((n, LANES), F32),
        ],
        scratch_shapes=[pltpu.VMEM((tm, D_MODEL), BF16)],
        compiler_params=pltpu.CompilerParams(
            dimension_semantics=("arbitrary", "arbitrary"), vmem_limit_bytes=48 * MIB),
        name="inproj",
    )(x2d, g, w_cat, w_alow)


def _mixer_kernel(uc_ref, ub_ref, ux_ref, q_ref, k_ref, v_ref, r_ref, a_ref,
                  cw_ref, cb_ref, wa2_ref, ba_ref, ng_ref,
                  ya_ref, yb_ref, st_ref, ybuf_ref, *, tt):
    t = pl.program_id(1)

    @pl.when(t == 0)
    def _():
        st_ref[...] = jnp.zeros_like(st_ref)
        ybuf_ref[0:HALO, :] = jnp.zeros((HALO, CONV_WIDTH), F32)

    @pl.when(t > 0)
    def _():
        ybuf_ref[0:HALO, :] = ybuf_ref[tt:tt + HALO, :]

    y = uc_ref[...].astype(F32) * ux_ref[...].astype(F32)
    ybuf_ref[HALO:HALO + tt, :] = y
    y1 = ybuf_ref[HALO - 1:HALO - 1 + tt, :]
    y2 = ybuf_ref[HALO - 2:HALO - 2 + tt, :]
    z = cb_ref[...] + cw_ref[0:1, :] * y2 + cw_ref[1:2, :] * y1 + cw_ref[2:3, :] * y
    ya_ref[...] = (ub_ref[...].astype(F32) * z).astype(BF16)

    row = lax.broadcasted_iota(jnp.int32, (CHUNK, CHUNK), 0)
    col = lax.broadcasted_iota(jnp.int32, (CHUNK, CHUNK), 1)
    tri = (row >= col).astype(F32)
    scale = GLA_DK ** -0.5
    for c in range(tt // CHUNK):
        rows = slice(c * CHUNK, (c + 1) * CHUNK)
        zf = jnp.dot(a_ref[rows, :], wa2_ref[...], preferred_element_type=F32,
                     precision=lax.Precision.HIGHEST) + ba_ref[...]
        log_a = (jnp.minimum(zf, 0.0) - jnp.log(1.0 + jnp.exp(-jnp.abs(zf)))) * (1.0 / GLA_TAU)
        bcum = jnp.dot(tri, log_a, preferred_element_type=F32, precision=lax.Precision.HIGHEST)
        a_end = bcum[CHUNK - 1:CHUNK, :]
        k_t = (k_ref[rows, :].astype(F32) * jnp.exp(a_end - bcum)).astype(BF16)
        decay = jnp.exp(a_end)
        for h in range(GLA_HEADS):
            ks = slice(h * GLA_DK, (h + 1) * GLA_DK)
            vs = slice(h * GLA_DV, (h + 1) * GLA_DV)
            kv_t = lax.dot_general(v_ref[rows, vs], k_t[:, ks], (((0,), (0,)), ((), ())),
                                   preferred_element_type=F32)
            s_t = decay[:, ks] * st_ref[h] + kv_t
            st_ref[h] = s_t
            o = lax.dot_general(q_ref[rows, ks], s_t.astype(BF16), (((1,), (1,)), ((), ())),
                                preferred_element_type=F32) * scale
            o = o * lax.rsqrt(jnp.mean(o * o, axis=-1, keepdims=True) + EPS) * ng_ref[:, vs]
            yb_ref[rows, vs] = (o * jax.nn.silu(r_ref[rows, vs].astype(F32))).astype(BF16)


def _mixer(proj, a_low, conv_w, conv_b, w_a2p, b_a, norm_g, *, batch, seq, tt=256):
    n = proj.shape[0]
    nt = seq // tt

    def col(width, start):
        return pl.BlockSpec((tt, width), lambda b, t: (b * nt + t, start // width))

    def whole(shape):
        return pl.BlockSpec(shape, lambda b, t: (0,) * len(shape))

    return pl.pallas_call(
        functools.partial(_mixer_kernel, tt=tt),
        grid=(batch, nt),
        in_specs=[
            col(CONV_WIDTH, COL_UC), col(CONV_WIDTH, COL_UB), col(CONV_WIDTH, COL_UX),
            col(GLA_HEADS * GLA_DK, COL_Q), col(GLA_HEADS * GLA_DK, COL_K),
            col(GLA_HEADS * GLA_DV, COL_V), col(GLA_HEADS * GLA_DV, COL_R),
            pl.BlockSpec((tt, LANES), lambda b, t: (b * nt + t, 0)),
            whole((CONV_K, CONV_WIDTH)), whole((1, CONV_WIDTH)),
            whole((LANES, GLA_HEADS * GLA_DK)), whole((1, GLA_HEADS * GLA_DK)),
            whole((1, GLA_HEADS * GLA_DV)),
        ],
        out_specs=[
            pl.BlockSpec((tt, CONV_WIDTH), lambda b, t: (b * nt + t, 0)),
            pl.BlockSpec((tt, GLA_HEADS * GLA_DV), lambda b, t: (b * nt + t, 0)),
        ],
        out_shape=[
            jax.ShapeDtypeStruct((n, CONV_WIDTH), BF16),
            jax.ShapeDtypeStruct((n, GLA_HEADS * GLA_DV), BF16),
        ],
        scratch_shapes=[
            pltpu.VMEM((GLA_HEADS, GLA_DV, GLA_DK), F32),
            pltpu.VMEM((tt + HALO, CONV_WIDTH), F32),
        ],
        compiler_params=pltpu.CompilerParams(
            dimension_semantics=("arbitrary", "arbitrary"), vmem_limit_bytes=32 * MIB),
        name="mixer",
    )(proj, proj, proj, proj, proj, proj, proj, a_low, conv_w, conv_b, w_a2p, b_a, norm_g)


def _proj_kernel(ya_ref, yb_ref, ga_ref, gb_ref, x_ref, wa_ref, wb_ref, wo_ref, ng_ref, wr_ref, br_ref,
                 x1_ref, h2_ref, ti_ref, tw_ref, rk_ref, cnt_ref):
    @pl.when(pl.program_id(0) == 0)
    def _():
        cnt_ref[...] = jnp.zeros_like(cnt_ref)

    y_a = jnp.dot(ya_ref[...], wa_ref[...], preferred_element_type=F32)
    y_b = jnp.dot(yb_ref[...], wb_ref[...], preferred_element_type=F32)
    mixed = (jax.nn.sigmoid(ga_ref[...].astype(F32)) * y_a
             + jax.nn.sigmoid(gb_ref[...].astype(F32)) * y_b)
    x1 = x_ref[...] + jnp.dot(mixed.astype(BF16), wo_ref[...], preferred_element_type=F32)
    x1_ref[...] = x1
    h2 = _rms(x1, ng_ref[...])
    h2_ref[...] = h2

    logits = jnp.dot(h2, wr_ref[...], preferred_element_type=F32,
                     precision=lax.Precision.HIGHEST) + br_ref[...]
    lane = lax.broadcasted_iota(jnp.int32, logits.shape, 1)
    neg = jnp.float32(-jnp.inf)
    work = jnp.where(lane < N_EXPERTS, logits, neg)
    top_i = jnp.zeros(logits.shape, jnp.int32)
    top_e = jnp.zeros(logits.shape, F32)
    v0 = None
    picks = []
    for kk in range(TOP_K):
        m = jnp.max(work, axis=-1, keepdims=True)
        idx = jnp.min(jnp.where(work == m, lane, LANES), axis=-1, keepdims=True)
        if kk == 0:
            v0 = m
        pick = lane == idx
        picks.append(pick)
        top_i = jnp.where(lane == kk, idx, top_i)
        top_e = jnp.where(lane == kk, jnp.exp(m - v0), top_e)
        work = jnp.where(pick, neg, work)
    ti_ref[...] = top_i
    tw_ref[...] = top_e / jnp.sum(top_e, axis=-1, keepdims=True)

    tm = logits.shape[0]
    sel = sum(p.astype(F32) for p in picks)
    earlier = (lax.broadcasted_iota(jnp.int32, (tm, tm), 0) > lax.broadcasted_iota(jnp.int32, (tm, tm), 1))
    before = cnt_ref[...] + jnp.dot(earlier.astype(BF16), sel.astype(BF16), preferred_element_type=F32)
    rank = jnp.zeros(logits.shape, jnp.int32)
    for kk in range(TOP_K):
        r_k = jnp.sum(jnp.where(picks[kk], before, 0.0), axis=-1, keepdims=True).astype(jnp.int32)
        rank = jnp.where(lane == kk, r_k, rank)
    rk_ref[...] = rank
    cnt_ref[...] = cnt_ref[...] + jnp.sum(sel, axis=0, keepdims=True)


def _proj(ya_in, yb_in, proj, x2d, w_br_a, w_br_b, w_out, norm_g, w_router, b_router, *, tm=256):
    n = x2d.shape[0]

    def rows(width, cblk=0):
        return pl.BlockSpec((tm, width), lambda i: (i, cblk))

    def whole(shape):
        return pl.BlockSpec(shape, lambda i: (0,) * len(shape), pipeline_mode=pl.Buffered(1))

    return pl.pallas_call(
        _proj_kernel,
        grid=(n // tm,),
        in_specs=[
            rows(CONV_WIDTH), rows(GLA_HEADS * GLA_DV),
            rows(D_MODEL, COL_GA // D_MODEL), rows(D_MODEL, COL_GB // D_MODEL),
            rows(D_MODEL),
            whole((CONV_WIDTH, D_MODEL)), whole((GLA_HEADS * GLA_DV, D_MODEL)), whole((D_MODEL, D_MODEL)),
            whole((1, D_MODEL)), whole((D_MODEL, LANES)), whole((1, LANES)),
        ],
        out_specs=[rows(D_MODEL), rows(D_MODEL), rows(LANES), rows(LANES), rows(LANES),
                   pl.BlockSpec((1, LANES), lambda i: (0, 0))],
        out_shape=[
            jax.ShapeDtypeStruct((n, D_MODEL), F32),
            jax.ShapeDtypeStruct((n, D_MODEL), F32),
            jax.ShapeDtypeStruct((n, LANES), jnp.int32),
            jax.ShapeDtypeStruct((n, LANES), F32),
            jax.ShapeDtypeStruct((n, LANES), jnp.int32),
            jax.ShapeDtypeStruct((1, LANES), F32),
        ],
        compiler_params=pltpu.CompilerParams(
            dimension_semantics=("arbitrary",), vmem_limit_bytes=56 * MIB),
        name="proj",
    )(ya_in, yb_in, proj, proj, x2d, w_br_a, w_br_b, w_out, norm_g, w_router, b_router)


def _lane_sum(x):
    return jnp.sum(x, axis=-1, keepdims=True)


def _route_kernel(ti_ref, rk_ref, cnt_ref, dest_ref, te_ref, meta_ref, *, log_tm, n_te):
    tm = 1 << log_tm
    lane8 = lax.broadcasted_iota(jnp.int32, (HALO, LANES), 1)
    cnt = jnp.broadcast_to(cnt_ref[...], (HALO, LANES)).astype(jnp.int32)
    padded = ((cnt + (tm - 1)) >> log_tm) << log_tm
    pad_end = padded
    for s in (1, 2, 4, 8, 16):
        pad_end = pad_end + jnp.where(lane8 >= s, pltpu.roll(pad_end, s, axis=1), 0)
    pad_start = pad_end - padded

    @pl.when(pl.program_id(0) == 0)
    def _():
        tile_start = lax.broadcasted_iota(jnp.int32, (n_te, LANES), 0) * tm
        lane_t = lax.broadcasted_iota(jnp.int32, (n_te, LANES), 1)
        done = (pad_end[0:1, :] <= tile_start) & (lane_t < N_EXPERTS)
        tile_e = jnp.minimum(_lane_sum(done.astype(F32)), N_EXPERTS - 1.0)
        te_ref[...] = jnp.broadcast_to(tile_e, (n_te, LANES)).astype(jnp.int32)
        total = _lane_sum(jnp.where(lane8 == N_EXPERTS - 1, pad_end, 0).astype(F32)).astype(jnp.int32)
        row8 = lax.broadcasted_iota(jnp.int32, (HALO, LANES), 0)
        meta_ref[...] = jnp.where(row8 == 0, total >> log_tm, jnp.where(row8 == 1, pad_end, cnt))

    lane = lax.broadcasted_iota(jnp.int32, ti_ref.shape, 1)
    top_f = ti_ref[...].astype(F32)
    start_f = pad_start[0:1, :].astype(F32)
    dest = rk_ref[...]
    for kk in range(TOP_K):
        e_k = _lane_sum(jnp.where(lane == kk, top_f, 0.0)).astype(jnp.int32)
        start_k = _lane_sum(jnp.where(lane == e_k, start_f, 0.0)).astype(jnp.int32)
        dest = dest + jnp.where(lane == kk, start_k, 0)
    dest_ref[...] = dest


def _route(top_i, rank, counts, *, log_tm, n_te, tr=2048):
    n = top_i.shape[0]
    return pl.pallas_call(
        functools.partial(_route_kernel, log_tm=log_tm, n_te=n_te),
        grid=(n // tr,),
        in_specs=[
            pl.BlockSpec((tr, LANES), lambda i: (i, 0)),
            pl.BlockSpec((tr, LANES), lambda i: (i, 0)),
            pl.BlockSpec((1, LANES), lambda i: (0, 0)),
        ],
        out_specs=[
            pl.BlockSpec((tr, LANES), lambda i: (i, 0)),
            pl.BlockSpec((n_te, LANES), lambda i: (0, 0)),
            pl.BlockSpec((HALO, LANES), lambda i: (0, 0)),
        ],
        out_shape=[
            jax.ShapeDtypeStruct((n, LANES), jnp.int32),
            jax.ShapeDtypeStruct((n_te, LANES), jnp.int32),
            jax.ShapeDtypeStruct((HALO, LANES), jnp.int32),
        ],
        compiler_params=pltpu.CompilerParams(dimension_semantics=("arbitrary",)),
        name="route",
    )(top_i, rank, counts)


def _scatter_kernel(pe_ref, cnt_ref, nu_ref, dest_ref, hp_ref, xs_ref, zero_ref, sem, *, tc, tm, n_tiles):
    @pl.when(pl.program_id(0) == 0)
    def _():
        zero_ref[...] = jnp.zeros_like(zero_ref)

        def last_tile(e):
            return pltpu.make_async_copy(
                zero_ref, xs_ref.at[pl.ds(pl.multiple_of(pe_ref[e] - tm, tm), tm)], sem.at[0])

        def tail_tile(j):
            return pltpu.make_async_copy(zero_ref, xs_ref.at[pl.ds(pl.multiple_of(j * tm, tm), tm)], sem.at[0])

        def start_tail(j, carry):
            tail_tile(j).start()
            return carry

        def wait_tail(j, carry):
            tail_tile(j).wait()
            return carry

        for e in range(N_EXPERTS):
            @pl.when(cnt_ref[e] > 0)
            def _():
                last_tile(e).start()
        lax.fori_loop(nu_ref[0], n_tiles, start_tail, 0)
        for e in range(N_EXPERTS):
            @pl.when(cnt_ref[e] > 0)
            def _():
                last_tile(e).wait()
        lax.fori_loop(nu_ref[0], n_tiles, wait_tail, 0)

    def issue(t, carry):
        for kk in range(TOP_K):
            pltpu.make_async_copy(hp_ref.at[pl.ds(t, 1)],
                                  xs_ref.at[pl.ds(dest_ref[t * TOP_K + kk], 1)], sem.at[1]).start()
        return carry

    lax.fori_loop(0, tc, issue, 0, unroll=2)
    for kk in range(TOP_K):
        pltpu.make_async_copy(hp_ref, xs_ref.at[pl.ds(0, tc)], sem.at[1]).wait()


def _scatter(pad_end, cnt, n_used, dest_flat, hp, *, n_slots, tm, tc=256):
    n = hp.shape[0]
    return pl.pallas_call(
        functools.partial(_scatter_kernel, tc=tc, tm=tm, n_tiles=n_slots // tm),
        grid_spec=pltpu.PrefetchScalarGridSpec(
            num_scalar_prefetch=3,
            grid=(n // tc,),
            in_specs=[
                pl.BlockSpec((TOP_K * tc,), lambda i, pe, ct, nu: (i,), memory_space=pltpu.SMEM),
                pl.BlockSpec((tc, D_MODEL), lambda i, pe, ct, nu: (i, 0)),
            ],
            out_specs=pl.BlockSpec(memory_space=pl.ANY),
            scratch_shapes=[pltpu.VMEM((tm, D_MODEL), F32), pltpu.SemaphoreType.DMA((2,))],
        ),
        out_shape=jax.ShapeDtypeStruct((n_slots, D_MODEL), F32),
        compiler_params=pltpu.CompilerParams(dimension_semantics=("arbitrary",)),
        name="scatter",
    )(pad_end, cnt, n_used, dest_flat, hp)


def _experts_kernel(te_ref, nu_ref, x_ref, wg_ref, wu_ref, wd_ref, bg_ref, bu_ref, bd_ref,
                    out_ref, xb_ref, *, n_f):
    i = pl.program_id(0)
    f = pl.program_id(1)

    @pl.when((i >= nu_ref[0]) & (f == n_f - 1))
    def _():
        out_ref[...] = jnp.zeros_like(out_ref)

    @pl.when(i < nu_ref[0])
    def _():
        @pl.when(f == 0)
        def _():
            xb_ref[...] = x_ref[...].astype(BF16)

        x = xb_ref[...]
        gate = jnp.dot(x, wg_ref[...].astype(BF16), preferred_element_type=F32) + bg_ref[...]
        up = jnp.dot(x, wu_ref[...].astype(BF16), preferred_element_type=F32) + bu_ref[...]
        gate = jnp.minimum(gate, SWIGLU_LIMIT)
        up = jnp.clip(up, -SWIGLU_LIMIT, SWIGLU_LIMIT)
        act = (up + 1.0) * (gate * jax.nn.sigmoid(SWIGLU_ALPHA * gate))
        contrib = jnp.dot(act.astype(BF16), wd_ref[...].astype(BF16), preferred_element_type=F32)

        @pl.when(f == 0)
        def _():
            out_ref[...] = contrib + bd_ref[...]

        @pl.when(f > 0)
        def _():
            out_ref[...] += contrib


def _experts(tile_e, n_used, xs, w_gate_up, w_down, b_gate_up, b_down, *, tm, tf=256):
    n_slots = xs.shape[0]
    n_f = D_FF // tf

    def tile(i, nu):
        return jnp.maximum(jnp.minimum(i, nu[0] - 1), 0)

    def fblk(i, f, nu):
        return jnp.where(i < nu[0], f, n_f - 1)

    return pl.pallas_call(
        functools.partial(_experts_kernel, n_f=n_f),
        grid_spec=pltpu.PrefetchScalarGridSpec(
            num_scalar_prefetch=2,
            grid=(n_slots // tm, n_f),
            in_specs=[
                pl.BlockSpec((tm, D_MODEL), lambda i, f, te, nu: (tile(i, nu), 0)),
                pl.BlockSpec((None, D_MODEL, tf), lambda i, f, te, nu: (te[tile(i, nu)], 0, fblk(i, f, nu))),
                pl.BlockSpec((None, D_MODEL, tf),
                             lambda i, f, te, nu: (te[tile(i, nu)], 0, n_f + fblk(i, f, nu))),
                pl.BlockSpec((None, tf, D_MODEL), lambda i, f, te, nu: (te[tile(i, nu)], fblk(i, f, nu), 0)),
                pl.BlockSpec((None, 1, tf), lambda i, f, te, nu: (te[tile(i, nu)], 0, fblk(i, f, nu))),
                pl.BlockSpec((None, 1, tf), lambda i, f, te, nu: (te[tile(i, nu)], 0, n_f + fblk(i, f, nu))),
                pl.BlockSpec((None, 1, D_MODEL), lambda i, f, te, nu: (te[tile(i, nu)], 0, 0)),
            ],
            out_specs=pl.BlockSpec((tm, D_MODEL), lambda i, f, te, nu: (i, 0)),
            scratch_shapes=[pltpu.VMEM((tm, D_MODEL), BF16)],
        ),
        out_shape=jax.ShapeDtypeStruct((n_slots, D_MODEL), F32),
        compiler_params=pltpu.CompilerParams(
            dimension_semantics=("arbitrary", "arbitrary"), vmem_limit_bytes=48 * MIB),
        name="experts",
    )(tile_e, n_used, xs, w_gate_up, w_gate_up, w_down, b_gate_up, b_gate_up, b_down)


def _combine_kernel(dest_ref, dnext_ref, tw_ref, x1_ref, g_ref, yb_ref, out_ref, buf_ref, sem, *, tc, n_tiles):
    i = pl.program_id(0)
    slot = i % 2

    def issue(idx_ref, s):
        def body(t, carry):
            for kk in range(TOP_K):
                pltpu.make_async_copy(yb_ref.at[pl.ds(idx_ref[t * TOP_K + kk], 1)],
                                      buf_ref.at[s, pl.ds(kk * tc + t, 1)], sem.at[s]).start()
            return carry
        lax.fori_loop(0, tc, body, 0, unroll=2)

    @pl.when(i == 0)
    def _():
        issue(dest_ref, 0)

    @pl.when(i + 1 < n_tiles)
    def _():
        issue(dnext_ref, 1 - slot)

    pltpu.make_async_copy(yb_ref.at[pl.ds(0, TOP_K * tc)], buf_ref.at[slot], sem.at[slot]).wait()

    lane = lax.broadcasted_iota(jnp.int32, (tc, LANES), 1)
    tw = tw_ref[...]
    x2 = x1_ref[...]
    for kk in range(TOP_K):
        w = jnp.sum(jnp.where(lane == kk, tw, 0.0), axis=-1, keepdims=True)
        x2 = x2 + w * buf_ref[slot, kk * tc:(kk + 1) * tc, :]
    out_ref[...] = _rms(x2, g_ref[...])


def _combine(dest_flat, top_w, x1, norm_g, yb, *, tc=256):
    n = x1.shape[0]
    n_tiles = n // tc
    return pl.pallas_call(
        functools.partial(_combine_kernel, tc=tc, n_tiles=n_tiles),
        grid=(n_tiles,),
        in_specs=[
            pl.BlockSpec((TOP_K * tc,), lambda i: (i,), memory_space=pltpu.SMEM),
            pl.BlockSpec((TOP_K * tc,), lambda i: (jnp.minimum(i + 1, n_tiles - 1),), memory_space=pltpu.SMEM),
            pl.BlockSpec((tc, LANES), lambda i: (i, 0)),
            pl.BlockSpec((tc, D_MODEL), lambda i: (i, 0)),
            pl.BlockSpec((1, D_MODEL), lambda i: (0, 0)),
            pl.BlockSpec(memory_space=pl.ANY),
        ],
        out_specs=pl.BlockSpec((tc, D_MODEL), lambda i: (i, 0)),
        out_shape=jax.ShapeDtypeStruct((n, D_MODEL), F32),
        scratch_shapes=[pltpu.VMEM((2, TOP_K * tc, D_MODEL), F32), pltpu.SemaphoreType.DMA((2,))],
        compiler_params=pltpu.CompilerParams(
            dimension_semantics=("arbitrary",), vmem_limit_bytes=40 * MIB),
        name="combine",
    )(dest_flat, dest_flat, top_w, x1, norm_g, yb)


def kernel(x, norm_mix_g, w_in, conv_w, conv_b, w_a2, b_a, gla_norm_g, w_br_a, w_br_b, w_out, norm_ffn_g,
           w_router, b_router, w_gate_up, b_gate_up, w_down, b_down, norm_final_g):
    bn, t, d = x.shape
    n = bn * t
    log_tm_e = 9
    tm_e = 1 << log_tm_e
    n_slots = n * TOP_K + N_EXPERTS * tm_e
    n_tiles = n_slots // tm_e
    x2d = x.reshape(n, d)
    assert w_in.shape[0] == 1, "single-layer block: the final norm is fused into the combine step"
    for l in range(1):
        wl = w_in[l]
        w_cat = jnp.concatenate([wl[:, :A_LOW_START], wl[:, A_LOW_START + GLA_LOWRANK:]], axis=1).astype(BF16)
        w_alow = jnp.pad(wl[:, A_LOW_START:A_LOW_START + GLA_LOWRANK],
                         ((0, 0), (0, LANES - GLA_LOWRANK))).astype(BF16)
        proj, a_low = _inproj(x2d, norm_mix_g[l][None, :], w_cat, w_alow)

        w_a2p = jnp.pad(w_a2[l], ((0, LANES - GLA_LOWRANK), (0, 0)))
        ya_in, yb_in = _mixer(proj, a_low, conv_w[l], conv_b[l][None, :], w_a2p, b_a[l][None, :],
                              gla_norm_g[l].reshape(1, -1), batch=bn, seq=t)

        w_rp = jnp.pad(w_router[l], ((0, 0), (0, LANES - N_EXPERTS)))
        b_rp = jnp.pad(b_router[l], (0, LANES - N_EXPERTS))[None, :]
        x1, hp, top_i, top_w, rank, counts = _proj(
            ya_in, yb_in, proj, x2d, w_br_a[l].astype(BF16), w_br_b[l].astype(BF16),
            w_out[l].astype(BF16), norm_ffn_g[l][None, :], w_rp, b_rp)

        dest, te, meta = _route(top_i, rank, counts, log_tm=log_tm_e, n_te=pl.cdiv(n_tiles, HALO) * HALO)
        tile_e = te[:n_tiles, 0]
        n_used = meta[0, :1]
        dest_flat = dest[:, :TOP_K].reshape(-1)
        xs = _scatter(meta[1, :N_EXPERTS], meta[2, :N_EXPERTS], n_used, dest_flat, hp, n_slots=n_slots, tm=tm_e)
        yb = _experts(tile_e, n_used, xs, w_gate_up[l], w_down[l],
                      b_gate_up[l][:, None, :], b_down[l][:, None, :], tm=tm_e)
        x2d = _combine(dest_flat, top_w, x1, norm_final_g[None, :], yb)
    return x2d.reshape(bn, t, d)
```

```python
import functools

import jax
import jax.numpy as jnp
from jax import lax
from jax.experimental import pallas as pl
from jax.experimental.pallas import tpu as pltpu

F32 = jnp.float32
BF16 = jnp.bfloat16

D_MODEL = 2048
CHUNK = 64
EPS = 1e-6
CONV_WIDTH = 1024
CONV_K = 3
GLA_HEADS = 4
GLA_DK = 128
GLA_DV = 256
GLA_LOWRANK = 16
GLA_TAU = 16.0
N_EXPERTS = 32
TOP_K = 4
D_FF = 2048
SWIGLU_LIMIT = 7.0
SWIGLU_ALPHA = 1.702

LANES = 128
HALO = 8

COL_UC, COL_UB, COL_UX = 0, 1024, 2048
COL_Q, COL_K, COL_V, COL_R = 3072, 3584, 4096, 5120
COL_GA, COL_GB = 6144, 8192
N_PROJ = 10240
A_LOW_START = 6144

MIB = 1024 * 1024
DOWN_CHUNK = 512


def _rms(x, g):
    return x * lax.rsqrt(jnp.mean(x * x, axis=-1, keepdims=True) + EPS) * g


def _inproj_kernel(x_ref, g_ref, w_ref, wa_ref, p_ref, a_ref, h_ref):
    @pl.when(pl.program_id(1) == 0)
    def _():
        hb = _rms(x_ref[...], g_ref[...]).astype(BF16)
        h_ref[...] = hb
        a_ref[...] = jnp.dot(hb, wa_ref[...], preferred_element_type=F32)

    p_ref[...] = jnp.dot(h_ref[...], w_ref[...], preferred_element_type=F32).astype(BF16)


def _inproj(x2d, g, w_cat, w_alow, *, tm=1024, tn=1024):
    n = x2d.shape[0]
    return pl.pallas_call(
        _inproj_kernel,
        grid=(n // tm, N_PROJ // tn),
        in_specs=[
            pl.BlockSpec((tm, D_MODEL), lambda m, j: (m, 0)),
            pl.BlockSpec((1, D_MODEL), lambda m, j: (0, 0)),
            pl.BlockSpec((D_MODEL, tn), lambda m, j: (0, j)),
            pl.BlockSpec((D_MODEL, LANES), lambda m, j: (0, 0)),
        ],
        out_specs=[
            pl.BlockSpec((tm, tn), lambda m, j: (m, j)),
            pl.BlockSpec((tm, LANES), lambda m, j: (m, 0)),
        ],
        out_shape=[
            jax.ShapeDtypeStruct((n, N_PROJ), BF16),
            jax.ShapeDtypeStruct((n, LANES), F32),
        ],
        scratch_shapes=[pltpu.VMEM((tm, D_MODEL), BF16)],
        compiler_params=pltpu.CompilerParams(
            dimension_semantics=("arbitrary", "arbitrary"), vmem_limit_bytes=48 * MIB),
        name="inproj",
    )(x2d, g, w_cat, w_alow)


def _mixer_kernel(uc_ref, ub_ref, ux_ref, q_ref, k_ref, v_ref, r_ref, a_ref,
                  cw_ref, cb_ref, wa2_ref, ba_ref, ng_ref,
                  ya_ref, yb_ref, st_ref, ybuf_ref, *, tt):
    t = pl.program_id(1)

    @pl.when(t == 0)
    def _():
        st_ref[...] = jnp.zeros_like(st_ref)
        ybuf_ref[0:HALO, :] = jnp.zeros((HALO, CONV_WIDTH), F32)

    @pl.when(t > 0)
    def _():
        ybuf_ref[0:HALO, :] = ybuf_ref[tt:tt + HALO, :]

    y = uc_ref[...].astype(F32) * ux_ref[...].astype(F32)
    ybuf_ref[HALO:HALO + tt, :] = y
    y1 = ybuf_ref[HALO - 1:HALO - 1 + tt, :]
    y2 = ybuf_ref[HALO - 2:HALO - 2 + tt, :]
    z = cb_ref[...] + cw_ref[0:1, :] * y2 + cw_ref[1:2, :] * y1 + cw_ref[2:3, :] * y
    ya_ref[...] = (ub_ref[...].astype(F32) * z).astype(BF16)

    row = lax.broadcasted_iota(jnp.int32, (CHUNK, CHUNK), 0)
    col = lax.broadcasted_iota(jnp.int32, (CHUNK, CHUNK), 1)
    tri = (row >= col).astype(F32)
    scale = GLA_DK ** -0.5
    for c in range(tt // CHUNK):
        rows = slice(c * CHUNK, (c + 1) * CHUNK)
        zf = jnp.dot(a_ref[rows, :], wa2_ref[...], preferred_element_type=F32,
                     precision=lax.Precision.HIGHEST) + ba_ref[...]
        log_a = (jnp.minimum(zf, 0.0) - jnp.log(1.0 + jnp.exp(-jnp.abs(zf)))) * (1.0 / GLA_TAU)
        bcum = jnp.dot(tri, log_a, preferred_element_type=F32, precision=lax.Precision.HIGHEST)
        a_end = bcum[CHUNK - 1:CHUNK, :]
        k_t = (k_ref[rows, :].astype(F32) * jnp.exp(a_end - bcum)).astype(BF16)
        decay = jnp.exp(a_end)
        for h in range(GLA_HEADS):
            ks = slice(h * GLA_DK, (h + 1) * GLA_DK)
            vs = slice(h * GLA_DV, (h + 1) * GLA_DV)
            kv_t = lax.dot_general(v_ref[rows, vs], k_t[:, ks], (((0,), (0,)), ((), ())),
                                   preferred_element_type=F32)
            s_t = decay[:, ks] * st_ref[h] + kv_t
            st_ref[h] = s_t
            o = lax.dot_general(q_ref[rows, ks], s_t.astype(BF16), (((1,), (1,)), ((), ())),
                                preferred_element_type=F32) * scale
            o = o * lax.rsqrt(jnp.mean(o * o, axis=-1, keepdims=True) + EPS) * ng_ref[:, vs]
            yb_ref[rows, vs] = (o * jax.nn.silu(r_ref[rows, vs].astype(F32))).astype(BF16)


def _mixer(proj, a_low, conv_w, conv_b, w_a2p, b_a, norm_g, *, batch, seq, tt=256):
    n = proj.shape[0]
    nt = seq // tt

    def col(width, start):
        return pl.BlockSpec((tt, width), lambda b, t: (b * nt + t, start // width))

    def whole(shape):
        return pl.BlockSpec(shape, lambda b, t: (0,) * len(shape))

    return pl.pallas_call(
        functools.partial(_mixer_kernel, tt=tt),
        grid=(batch, nt),
        in_specs=[
            col(CONV_WIDTH, COL_UC), col(CONV_WIDTH, COL_UB), col(CONV_WIDTH, COL_UX),
            col(GLA_HEADS * GLA_DK, COL_Q), col(GLA_HEADS * GLA_DK, COL_K),
            col(GLA_HEADS * GLA_DV, COL_V), col(GLA_HEADS * GLA_DV, COL_R),
            pl.BlockSpec((tt, LANES), lambda b, t: (b * nt + t, 0)),
            whole((CONV_K, CONV_WIDTH)), whole((1, CONV_WIDTH)),
            whole((LANES, GLA_HEADS * GLA_DK)), whole((1, GLA_HEADS * GLA_DK)),
            whole((1, GLA_HEADS * GLA_DV)),
        ],
        out_specs=[
            pl.BlockSpec((tt, CONV_WIDTH), lambda b, t: (b * nt + t, 0)),
            pl.BlockSpec((tt, GLA_HEADS * GLA_DV), lambda b, t: (b * nt + t, 0)),
        ],
        out_shape=[
            jax.ShapeDtypeStruct((n, CONV_WIDTH), BF16),
            jax.ShapeDtypeStruct((n, GLA_HEADS * GLA_DV), BF16),
        ],
        scratch_shapes=[
            pltpu.VMEM((GLA_HEADS, GLA_DV, GLA_DK), F32),
            pltpu.VMEM((tt + HALO, CONV_WIDTH), F32),
        ],
        compiler_params=pltpu.CompilerParams(
            dimension_semantics=("arbitrary", "arbitrary"), vmem_limit_bytes=32 * MIB),
        name="mixer",
    )(proj, proj, proj, proj, proj, proj, proj, a_low, conv_w, conv_b, w_a2p, b_a, norm_g)


def _proj_kernel(ya_ref, yb_ref, ga_ref, gb_ref, x_ref, wa_ref, wb_ref, wo_ref, ng_ref, wr_ref, br_ref,
                 x1_ref, h2_ref, ti_ref, tw_ref, rk_ref, cnt_ref):
    @pl.when(pl.program_id(0) == 0)
    def _():
        cnt_ref[...] = jnp.zeros_like(cnt_ref)

    y_a = jnp.dot(ya_ref[...], wa_ref[...], preferred_element_type=F32)
    y_b = jnp.dot(yb_ref[...], wb_ref[...], preferred_element_type=F32)
    mixed = (jax.nn.sigmoid(ga_ref[...].astype(F32)) * y_a
             + jax.nn.sigmoid(gb_ref[...].astype(F32)) * y_b)
    x1 = x_ref[...] + jnp.dot(mixed.astype(BF16), wo_ref[...], preferred_element_type=F32)
    x1_ref[...] = x1
    h2 = _rms(x1, ng_ref[...])
    h2_ref[...] = h2

    logits = jnp.dot(h2, wr_ref[...], preferred_element_type=F32,
                     precision=lax.Precision.HIGHEST) + br_ref[...]
    lane = lax.broadcasted_iota(jnp.int32, logits.shape, 1)
    neg = jnp.float32(-jnp.inf)
    work = jnp.where(lane < N_EXPERTS, logits, neg)
    top_i = jnp.zeros(logits.shape, jnp.int32)
    top_e = jnp.zeros(logits.shape, F32)
    v0 = None
    picks = []
    for kk in range(TOP_K):
        m = jnp.max(work, axis=-1, keepdims=True)
        idx = jnp.min(jnp.where(work == m, lane, LANES), axis=-1, keepdims=True)
        if kk == 0:
            v0 = m
        pick = lane == idx
        picks.append(pick)
        top_i = jnp.where(lane == kk, idx, top_i)
        top_e = jnp.where(lane == kk, jnp.exp(m - v0), top_e)
        work = jnp.where(pick, neg, work)
    ti_ref[...] = top_i
    tw_ref[...] = top_e / jnp.sum(top_e, axis=-1, keepdims=True)

    tm = logits.shape[0]
    sel = sum(p.astype(F32) for p in picks)
    earlier = (lax.broadcasted_iota(jnp.int32, (tm, tm), 0) > lax.broadcasted_iota(jnp.int32, (tm, tm), 1))
    before = cnt_ref[...] + jnp.dot(earlier.astype(BF16), sel.astype(BF16), preferred_element_type=F32)
    rank = jnp.zeros(logits.shape, jnp.int32)
    for kk in range(TOP_K):
        r_k = jnp.sum(jnp.where(picks[kk], before, 0.0), axis=-1, keepdims=True).astype(jnp.int32)
        rank = jnp.where(lane == kk, r_k, rank)
    rk_ref[...] = rank
    cnt_ref[...] = cnt_ref[...] + jnp.sum(sel, axis=0, keepdims=True)


def _proj(ya_in, yb_in, proj, x2d, w_br_a, w_br_b, w_out, norm_g, w_router, b_router, *, tm=256):
    n = x2d.shape[0]

    def rows(width, cblk=0):
        return pl.BlockSpec((tm, width), lambda i: (i, cblk))

    def whole(shape):
        return pl.BlockSpec(shape, lambda i: (0,) * len(shape), pipeline_mode=pl.Buffered(1))

    return pl.pallas_call(
        _proj_kernel,
        grid=(n // tm,),
        in_specs=[
            rows(CONV_WIDTH), rows(GLA_HEADS * GLA_DV),
            rows(D_MODEL, COL_GA // D_MODEL), rows(D_MODEL, COL_GB // D_MODEL),
            rows(D_MODEL),
            whole((CONV_WIDTH, D_MODEL)), whole((GLA_HEADS * GLA_DV, D_MODEL)), whole((D_MODEL, D_MODEL)),
            whole((1, D_MODEL)), whole((D_MODEL, LANES)), whole((1, LANES)),
        ],
        out_specs=[rows(D_MODEL), rows(D_MODEL), rows(LANES), rows(LANES), rows(LANES),
                   pl.BlockSpec((1, LANES), lambda i: (0, 0))],
        out_shape=[
            jax.ShapeDtypeStruct((n, D_MODEL), F32),
            jax.ShapeDtypeStruct((n, D_MODEL), F32),
            jax.ShapeDtypeStruct((n, LANES), jnp.int32),
            jax.ShapeDtypeStruct((n, LANES), F32),
            jax.ShapeDtypeStruct((n, LANES), jnp.int32),
            jax.ShapeDtypeStruct((1, LANES), F32),
        ],
        compiler_params=pltpu.CompilerParams(
            dimension_semantics=("arbitrary",), vmem_limit_bytes=56 * MIB),
        name="proj",
    )(ya_in, yb_in, proj, proj, x2d, w_br_a, w_br_b, w_out, norm_g, w_router, b_router)


def _lane_sum(x):
    return jnp.sum(x, axis=-1, keepdims=True)


def _route_kernel(ti_ref, rk_ref, cnt_ref, dest_ref, te_ref, meta_ref, *, log_tm, n_te):
    tm = 1 << log_tm
    lane8 = lax.broadcasted_iota(jnp.int32, (HALO, LANES), 1)
    cnt = jnp.broadcast_to(cnt_ref[...], (HALO, LANES)).astype(jnp.int32)
    padded = ((cnt + (tm - 1)) >> log_tm) << log_tm
    pad_end = padded
    for s in (1, 2, 4, 8, 16):
        pad_end = pad_end + jnp.where(lane8 >= s, pltpu.roll(pad_end, s, axis=1), 0)
    pad_start = pad_end - padded

    @pl.when(pl.program_id(0) == 0)
    def _():
        tile_start = lax.broadcasted_iota(jnp.int32, (n_te, LANES), 0) * tm
        lane_t = lax.broadcasted_iota(jnp.int32, (n_te, LANES), 1)
        done = (pad_end[0:1, :] <= tile_start) & (lane_t < N_EXPERTS)
        tile_e = jnp.minimum(_lane_sum(done.astype(F32)), N_EXPERTS - 1.0)
        te_ref[...] = jnp.broadcast_to(tile_e, (n_te, LANES)).astype(jnp.int32)
        total = _lane_sum(jnp.where(lane8 == N_EXPERTS - 1, pad_end, 0).astype(F32)).astype(jnp.int32)
        row8 = lax.broadcasted_iota(jnp.int32, (HALO, LANES), 0)
        meta_ref[...] = jnp.where(row8 == 0, total >> log_tm, jnp.where(row8 == 1, pad_end, cnt))

    lane = lax.broadcasted_iota(jnp.int32, ti_ref.shape, 1)
    top_f = ti_ref[...].astype(F32)
    start_f = pad_start[0:1, :].astype(F32)
    dest = rk_ref[...]
    for kk in range(TOP_K):
        e_k = _lane_sum(jnp.where(lane == kk, top_f, 0.0)).astype(jnp.int32)
        start_k = _lane_sum(jnp.where(lane == e_k, start_f, 0.0)).astype(jnp.int32)
        dest = dest + jnp.where(lane == kk, start_k, 0)
    dest_ref[...] = dest


def _route(top_i, rank, counts, *, log_tm, n_te, tr=2048):
    n = top_i.shape[0]
    return pl.pallas_call(
        functools.partial(_route_kernel, log_tm=log_tm, n_te=n_te),
        grid=(n // tr,),
        in_specs=[
            pl.BlockSpec((tr, LANES), lambda i: (i, 0)),
            pl.BlockSpec((tr, LANES), lambda i: (i, 0)),
            pl.BlockSpec((1, LANES), lambda i: (0, 0)),
        ],
        out_specs=[
            pl.BlockSpec((tr, LANES), lambda i: (i, 0)),
            pl.BlockSpec((n_te, LANES), lambda i: (0, 0)),
            pl.BlockSpec((HALO, LANES), lambda i: (0, 0)),
        ],
        out_shape=[
            jax.ShapeDtypeStruct((n, LANES), jnp.int32),
            jax.ShapeDtypeStruct((n_te, LANES), jnp.int32),
            jax.ShapeDtypeStruct((HALO, LANES), jnp.int32),
        ],
        compiler_params=pltpu.CompilerParams(dimension_semantics=("arbitrary",)),
        name="route",
    )(top_i, rank, counts)


def _scatter_kernel(pe_ref, cnt_ref, nu_ref, dest_ref, hp_ref, xs_ref, zero_ref, sem, *, tc, tm, n_tiles):
    @pl.when(pl.program_id(0) == 0)
    def _():
        zero_ref[...] = jnp.zeros_like(zero_ref)

        def last_tile(e):
            return pltpu.make_async_copy(
                zero_ref, xs_ref.at[pl.ds(pl.multiple_of(pe_ref[e] - tm, tm), tm)], sem.at[0])

        def tail_tile(j):
            return pltpu.make_async_copy(zero_ref, xs_ref.at[pl.ds(pl.multiple_of(j * tm, tm), tm)], sem.at[0])

        def start_tail(j, carry):
            tail_tile(j).start()
            return carry

        def wait_tail(j, carry):
            tail_tile(j).wait()
            return carry

        for e in range(N_EXPERTS):
            @pl.when(cnt_ref[e] > 0)
            def _():
                last_tile(e).start()
        lax.fori_loop(nu_ref[0], n_tiles, start_tail, 0)
        for e in range(N_EXPERTS):
            @pl.when(cnt_ref[e] > 0)
            def _():
                last_tile(e).wait()
        lax.fori_loop(nu_ref[0], n_tiles, wait_tail, 0)

    def issue(t, carry):
        for kk in range(TOP_K):
            pltpu.make_async_copy(hp_ref.at[pl.ds(t, 1)],
                                  xs_ref.at[pl.ds(dest_ref[t * TOP_K + kk], 1)], sem.at[1]).start()
        return carry

    lax.fori_loop(0, tc, issue, 0, unroll=2)
    for kk in range(TOP_K):
        pltpu.make_async_copy(hp_ref, xs_ref.at[pl.ds(0, tc)], sem.at[1]).wait()


def _scatter(pad_end, cnt, n_used, dest_flat, hp, *, n_slots, tm, tc=256):
    n = hp.shape[0]
    return pl.pallas_call(
        functools.partial(_scatter_kernel, tc=tc, tm=tm, n_tiles=n_slots // tm),
        grid_spec=pltpu.PrefetchScalarGridSpec(
            num_scalar_prefetch=3,
            grid=(n // tc,),
            in_specs=[
                pl.BlockSpec((TOP_K * tc,), lambda i, pe, ct, nu: (i,), memory_space=pltpu.SMEM),
                pl.BlockSpec((tc, D_MODEL), lambda i, pe, ct, nu: (i, 0)),
            ],
            out_specs=pl.BlockSpec(memory_space=pl.ANY),
            scratch_shapes=[pltpu.VMEM((tm, D_MODEL), F32), pltpu.SemaphoreType.DMA((2,))],
        ),
        out_shape=jax.ShapeDtypeStruct((n_slots, D_MODEL), F32),
        compiler_params=pltpu.CompilerParams(dimension_semantics=("arbitrary",), vmem_limit_bytes=32 * MIB),
        name="scatter",
    )(pad_end, cnt, n_used, dest_flat, hp)


def _experts_kernel(te_ref, nu_ref, x_ref, wg_ref, wu_ref, wd_ref, bg_ref, bu_ref, bd_ref,
                    out_ref, xb_ref, *, n_f):
    i = pl.program_id(0)
    f = pl.program_id(1)

    @pl.when((i >= nu_ref[0]) & (f == n_f - 1))
    def _():
        out_ref[...] = jnp.zeros_like(out_ref)

    @pl.when(i < nu_ref[0])
    def _():
        @pl.when(f == 0)
        def _():
            xb_ref[...] = x_ref[...].astype(BF16)

        x = xb_ref[...]
        gate = jnp.dot(x, wg_ref[...].astype(BF16), preferred_element_type=F32) + bg_ref[...]
        up = jnp.dot(x, wu_ref[...].astype(BF16), preferred_element_type=F32) + bu_ref[...]
        gate = jnp.minimum(gate, SWIGLU_LIMIT)
        up = jnp.clip(up, -SWIGLU_LIMIT, SWIGLU_LIMIT)
        act = (up + 1.0) * (gate * jax.nn.sigmoid(SWIGLU_ALPHA * gate))
        act = act.astype(BF16)

        @pl.when(f == 0)
        def _():
            out_ref[...] = jnp.broadcast_to(bd_ref[...], out_ref.shape)

        for c in range(D_MODEL // DOWN_CHUNK):
            cols = slice(c * DOWN_CHUNK, (c + 1) * DOWN_CHUNK)
            out_ref[:, cols] += jnp.dot(act, wd_ref[:, cols].astype(BF16), preferred_element_type=F32)


def _experts(tile_e, n_used, xs, w_gate_up, w_down, b_gate_up, b_down, *, tm, tf=256):
    n_slots = xs.shape[0]
    n_f = D_FF // tf

    def tile(i, nu):
        return jnp.maximum(jnp.minimum(i, nu[0] - 1), 0)

    def fblk(i, f, nu):
        return jnp.where(i < nu[0], f, n_f - 1)

    return pl.pallas_call(
        functools.partial(_experts_kernel, n_f=n_f),
        grid_spec=pltpu.PrefetchScalarGridSpec(
            num_scalar_prefetch=2,
            grid=(n_slots // tm, n_f),
            in_specs=[
                pl.BlockSpec((tm, D_MODEL), lambda i, f, te, nu: (tile(i, nu), 0), pipeline_mode=pl.Buffered(1)),
                pl.BlockSpec((None, D_MODEL, tf), lambda i, f, te, nu: (te[tile(i, nu)], 0, fblk(i, f, nu))),
                pl.BlockSpec((None, D_MODEL, tf),
                             lambda i, f, te, nu: (te[tile(i, nu)], 0, n_f + fblk(i, f, nu))),
                pl.BlockSpec((None, tf, D_MODEL), lambda i, f, te, nu: (te[tile(i, nu)], fblk(i, f, nu), 0)),
                pl.BlockSpec((None, 1, tf), lambda i, f, te, nu: (te[tile(i, nu)], 0, fblk(i, f, nu))),
                pl.BlockSpec((None, 1, tf), lambda i, f, te, nu: (te[tile(i, nu)], 0, n_f + fblk(i, f, nu))),
                pl.BlockSpec((None, 1, D_MODEL), lambda i, f, te, nu: (te[tile(i, nu)], 0, 0)),
            ],
            out_specs=pl.BlockSpec((tm, D_MODEL), lambda i, f, te, nu: (i, 0)),
            scratch_shapes=[pltpu.VMEM((tm, D_MODEL), BF16)],
        ),
        out_shape=jax.ShapeDtypeStruct((n_slots, D_MODEL), F32),
        compiler_params=pltpu.CompilerParams(
            dimension_semantics=("arbitrary", "arbitrary"), vmem_limit_bytes=56 * MIB),
        name="experts",
    )(tile_e, n_used, xs, w_gate_up, w_gate_up, w_down, b_gate_up, b_gate_up, b_down)


def _combine_kernel(dest_ref, dnext_ref, tw_ref, x1_ref, g_ref, yb_ref, out_ref, buf_ref, sem, *, tc, n_tiles):
    i = pl.program_id(0)
    slot = i % 2

    def issue(idx_ref, s):
        def body(t, carry):
            for kk in range(TOP_K):
                pltpu.make_async_copy(yb_ref.at[pl.ds(idx_ref[t * TOP_K + kk], 1)],
                                      buf_ref.at[s, pl.ds(kk * tc + t, 1)], sem.at[s]).start()
            return carry
        lax.fori_loop(0, tc, body, 0, unroll=2)

    @pl.when(i == 0)
    def _():
        issue(dest_ref, 0)

    @pl.when(i + 1 < n_tiles)
    def _():
        issue(dnext_ref, 1 - slot)

    pltpu.make_async_copy(yb_ref.at[pl.ds(0, TOP_K * tc)], buf_ref.at[slot], sem.at[slot]).wait()

    lane = lax.broadcasted_iota(jnp.int32, (tc, LANES), 1)
    tw = tw_ref[...]
    x2 = x1_ref[...]
    for kk in range(TOP_K):
        w = jnp.sum(jnp.where(lane == kk, tw, 0.0), axis=-1, keepdims=True)
        x2 = x2 + w * buf_ref[slot, kk * tc:(kk + 1) * tc, :]
    out_ref[...] = _rms(x2, g_ref[...])


def _combine(dest_flat, top_w, x1, norm_g, yb, *, tc=256):
    n = x1.shape[0]
    n_tiles = n // tc
    return pl.pallas_call(
        functools.partial(_combine_kernel, tc=tc, n_tiles=n_tiles),
        grid=(n_tiles,),
        in_specs=[
            pl.BlockSpec((TOP_K * tc,), lambda i: (i,), memory_space=pltpu.SMEM),
            pl.BlockSpec((TOP_K * tc,), lambda i: (jnp.minimum(i + 1, n_tiles - 1),), memory_space=pltpu.SMEM),
            pl.BlockSpec((tc, LANES), lambda i: (i, 0)),
            pl.BlockSpec((tc, D_MODEL), lambda i: (i, 0)),
            pl.BlockSpec((1, D_MODEL), lambda i: (0, 0)),
            pl.BlockSpec(memory_space=pl.ANY),
        ],
        out_specs=pl.BlockSpec((tc, D_MODEL), lambda i: (i, 0)),
        out_shape=jax.ShapeDtypeStruct((n, D_MODEL), F32),
        scratch_shapes=[pltpu.VMEM((2, TOP_K * tc, D_MODEL), F32), pltpu.SemaphoreType.DMA((2,))],
        compiler_params=pltpu.CompilerParams(
            dimension_semantics=("arbitrary",), vmem_limit_bytes=40 * MIB),
        name="combine",
    )(dest_flat, dest_flat, top_w, x1, norm_g, yb)


def kernel(x, norm_mix_g, w_in, conv_w, conv_b, w_a2, b_a, gla_norm_g, w_br_a, w_br_b, w_out, norm_ffn_g,
           w_router, b_router, w_gate_up, b_gate_up, w_down, b_down, norm_final_g):
    bn, t, d = x.shape
    n = bn * t
    log_tm_e = 10
    tm_e = 1 << log_tm_e
    n_slots = n * TOP_K + N_EXPERTS * tm_e
    n_tiles = n_slots // tm_e
    x2d = x.reshape(n, d)
    assert w_in.shape[0] == 1, "single-layer block: the final norm is fused into the combine step"
    for l in range(1):
        wl = w_in[l]
        w_cat = jnp.concatenate([wl[:, :A_LOW_START], wl[:, A_LOW_START + GLA_LOWRANK:]], axis=1).astype(BF16)
        w_alow = jnp.pad(wl[:, A_LOW_START:A_LOW_START + GLA_LOWRANK],
                         ((0, 0), (0, LANES - GLA_LOWRANK))).astype(BF16)
        proj, a_low = _inproj(x2d, norm_mix_g[l][None, :], w_cat, w_alow)

        w_a2p = jnp.pad(w_a2[l], ((0, LANES - GLA_LOWRANK), (0, 0)))
        ya_in, yb_in = _mixer(proj, a_low, conv_w[l], conv_b[l][None, :], w_a2p, b_a[l][None, :],
                              gla_norm_g[l].reshape(1, -1), batch=bn, seq=t)

        w_rp = jnp.pad(w_router[l], ((0, 0), (0, LANES - N_EXPERTS)))
        b_rp = jnp.pad(b_router[l], (0, LANES - N_EXPERTS))[None, :]
        x1, hp, top_i, top_w, rank, counts = _proj(
            ya_in, yb_in, proj, x2d, w_br_a[l].astype(BF16), w_br_b[l].astype(BF16),
            w_out[l].astype(BF16), norm_ffn_g[l][None, :], w_rp, b_rp)

        dest, te, meta = _route(top_i, rank, counts, log_tm=log_tm_e, n_te=pl.cdiv(n_tiles, HALO) * HALO)
        tile_e = te[:n_tiles, 0]
        n_used = meta[0, :1]
        dest_flat = dest[:, :TOP_K].reshape(-1)
        xs = _scatter(meta[1, :N_EXPERTS], meta[2, :N_EXPERTS], n_used, dest_flat, hp, n_slots=n_slots, tm=tm_e)
        yb = _experts(tile_e, n_used, xs, w_gate_up[l], w_down[l],
                      b_gate_up[l][:, None, :], b_down[l][:, None, :], tm=tm_e)
        x2d = _combine(dest_flat, top_w, x1, norm_final_g[None, :], yb)
    return x2d.reshape(bn, t, d)
```

```python
import functools

import jax
import jax.numpy as jnp
from jax import lax
from jax.experimental import pallas as pl
from jax.experimental.pallas import tpu as pltpu

F32 = jnp.float32
BF16 = jnp.bfloat16

D_MODEL = 2048
CHUNK = 64
EPS = 1e-6
CONV_WIDTH = 1024
CONV_K = 3
GLA_HEADS = 4
GLA_DK = 128
GLA_DV = 256
GLA_LOWRANK = 16
GLA_TAU = 16.0
N_EXPERTS = 32
TOP_K = 4
D_FF = 2048
SWIGLU_LIMIT = 7.0
SWIGLU_ALPHA = 1.702

LANES = 128
HALO = 8

COL_UC, COL_UB, COL_UX = 0, 1024, 2048
COL_Q, COL_K, COL_V, COL_R = 3072, 3584, 4096, 5120
COL_GA, COL_GB = 6144, 8192
N_PROJ = 10240
A_LOW_START = 6144

MIB = 1024 * 1024
DOWN_CHUNK = 512


def _rms(x, g):
    return x * lax.rsqrt(jnp.mean(x * x, axis=-1, keepdims=True) + EPS) * g


def _inproj_kernel(x_ref, g_ref, w_ref, wa_ref, p_ref, a_ref, h_ref):
    @pl.when(pl.program_id(1) == 0)
    def _():
        hb = _rms(x_ref[...], g_ref[...]).astype(BF16)
        h_ref[...] = hb
        a_ref[...] = jnp.dot(hb, wa_ref[...], preferred_element_type=F32)

    p_ref[...] = jnp.dot(h_ref[...], w_ref[...], preferred_element_type=F32).astype(BF16)


def _inproj(x2d, g, w_cat, w_alow, *, tm=1024, tn=1024):
    n = x2d.shape[0]
    return pl.pallas_call(
        _inproj_kernel,
        grid=(n // tm, N_PROJ // tn),
        in_specs=[
            pl.BlockSpec((tm, D_MODEL), lambda m, j: (m, 0)),
            pl.BlockSpec((1, D_MODEL), lambda m, j: (0, 0)),
            pl.BlockSpec((D_MODEL, tn), lambda m, j: (0, j)),
            pl.BlockSpec((D_MODEL, LANES), lambda m, j: (0, 0)),
        ],
        out_specs=[
            pl.BlockSpec((tm, tn), lambda m, j: (m, j)),
            pl.BlockSpec((tm, LANES), lambda m, j: (m, 0)),
        ],
        out_shape=[
            jax.ShapeDtypeStruct((n, N_PROJ), BF16),
            jax.ShapeDtypeStruct((n, LANES), F32),
        ],
        scratch_shapes=[pltpu.VMEM((tm, D_MODEL), BF16)],
        compiler_params=pltpu.CompilerParams(
            dimension_semantics=("arbitrary", "arbitrary"), vmem_limit_bytes=48 * MIB),
        name="inproj",
    )(x2d, g, w_cat, w_alow)


def _mixer_kernel(uc_ref, ub_ref, ux_ref, q_ref, k_ref, v_ref, r_ref, a_ref,
                  cw_ref, cb_ref, wa2_ref, ba_ref, ng_ref,
                  ya_ref, yb_ref, st_ref, ybuf_ref, *, tt):
    t = pl.program_id(1)

    @pl.when(t == 0)
    def _():
        st_ref[...] = jnp.zeros_like(st_ref)
        ybuf_ref[0:HALO, :] = jnp.zeros((HALO, CONV_WIDTH), F32)

    @pl.when(t > 0)
    def _():
        ybuf_ref[0:HALO, :] = ybuf_ref[tt:tt + HALO, :]

    y = uc_ref[...].astype(F32) * ux_ref[...].astype(F32)
    ybuf_ref[HALO:HALO + tt, :] = y
    y1 = ybuf_ref[HALO - 1:HALO - 1 + tt, :]
    y2 = ybuf_ref[HALO - 2:HALO - 2 + tt, :]
    z = cb_ref[...] + cw_ref[0:1, :] * y2 + cw_ref[1:2, :] * y1 + cw_ref[2:3, :] * y
    ya_ref[...] = (ub_ref[...].astype(F32) * z).astype(BF16)

    row = lax.broadcasted_iota(jnp.int32, (CHUNK, CHUNK), 0)
    col = lax.broadcasted_iota(jnp.int32, (CHUNK, CHUNK), 1)
    tri = (row >= col).astype(F32)
    scale = GLA_DK ** -0.5
    for c in range(tt // CHUNK):
        rows = slice(c * CHUNK, (c + 1) * CHUNK)
        zf = jnp.dot(a_ref[rows, :], wa2_ref[...], preferred_element_type=F32,
                     precision=lax.Precision.HIGHEST) + ba_ref[...]
        log_a = (jnp.minimum(zf, 0.0) - jnp.log(1.0 + jnp.exp(-jnp.abs(zf)))) * (1.0 / GLA_TAU)
        bcum = jnp.dot(tri, log_a, preferred_element_type=F32, precision=lax.Precision.HIGHEST)
        a_end = bcum[CHUNK - 1:CHUNK, :]
        k_t = (k_ref[rows, :].astype(F32) * jnp.exp(a_end - bcum)).astype(BF16)
        decay = jnp.exp(a_end)
        for h in range(GLA_HEADS):
            ks = slice(h * GLA_DK, (h + 1) * GLA_DK)
            vs = slice(h * GLA_DV, (h + 1) * GLA_DV)
            kv_t = lax.dot_general(v_ref[rows, vs], k_t[:, ks], (((0,), (0,)), ((), ())),
                                   preferred_element_type=F32)
            s_t = decay[:, ks] * st_ref[h] + kv_t
            st_ref[h] = s_t
            o = lax.dot_general(q_ref[rows, ks], s_t.astype(BF16), (((1,), (1,)), ((), ())),
                                preferred_element_type=F32) * scale
            o = o * lax.rsqrt(jnp.mean(o * o, axis=-1, keepdims=True) + EPS) * ng_ref[:, vs]
            yb_ref[rows, vs] = (o * jax.nn.silu(r_ref[rows, vs].astype(F32))).astype(BF16)


def _mixer(proj, a_low, conv_w, conv_b, w_a2p, b_a, norm_g, *, batch, seq, tt=256):
    n = proj.shape[0]
    nt = seq // tt

    def col(width, start):
        return pl.BlockSpec((tt, width), lambda b, t: (b * nt + t, start // width))

    def whole(shape):
        return pl.BlockSpec(shape, lambda b, t: (0,) * len(shape))

    return pl.pallas_call(
        functools.partial(_mixer_kernel, tt=tt),
        grid=(batch, nt),
        in_specs=[
            col(CONV_WIDTH, COL_UC), col(CONV_WIDTH, COL_UB), col(CONV_WIDTH, COL_UX),
            col(GLA_HEADS * GLA_DK, COL_Q), col(GLA_HEADS * GLA_DK, COL_K),
            col(GLA_HEADS * GLA_DV, COL_V), col(GLA_HEADS * GLA_DV, COL_R),
            pl.BlockSpec((tt, LANES), lambda b, t: (b * nt + t, 0)),
            whole((CONV_K, CONV_WIDTH)), whole((1, CONV_WIDTH)),
            whole((LANES, GLA_HEADS * GLA_DK)), whole((1, GLA_HEADS * GLA_DK)),
            whole((1, GLA_HEADS * GLA_DV)),
        ],
        out_specs=[
            pl.BlockSpec((tt, CONV_WIDTH), lambda b, t: (b * nt + t, 0)),
            pl.BlockSpec((tt, GLA_HEADS * GLA_DV), lambda b, t: (b * nt + t, 0)),
        ],
        out_shape=[
            jax.ShapeDtypeStruct((n, CONV_WIDTH), BF16),
            jax.ShapeDtypeStruct((n, GLA_HEADS * GLA_DV), BF16),
        ],
        scratch_shapes=[
            pltpu.VMEM((GLA_HEADS, GLA_DV, GLA_DK), F32),
            pltpu.VMEM((tt + HALO, CONV_WIDTH), F32),
        ],
        compiler_params=pltpu.CompilerParams(
            dimension_semantics=("arbitrary", "arbitrary"), vmem_limit_bytes=32 * MIB),
        name="mixer",
    )(proj, proj, proj, proj, proj, proj, proj, a_low, conv_w, conv_b, w_a2p, b_a, norm_g)


def _proj_kernel(ya_ref, yb_ref, ga_ref, gb_ref, x_ref, wa_ref, wb_ref, wo_ref, ng_ref, wr_ref, br_ref,
                 x1_ref, h2_ref, ti_ref, tw_ref, rk_ref, cnt_ref):
    @pl.when(pl.program_id(0) == 0)
    def _():
        cnt_ref[...] = jnp.zeros_like(cnt_ref)

    y_a = jnp.dot(ya_ref[...], wa_ref[...], preferred_element_type=F32)
    y_b = jnp.dot(yb_ref[...], wb_ref[...], preferred_element_type=F32)
    mixed = (jax.nn.sigmoid(ga_ref[...].astype(F32)) * y_a
             + jax.nn.sigmoid(gb_ref[...].astype(F32)) * y_b)
    x1 = x_ref[...] + jnp.dot(mixed.astype(BF16), wo_ref[...], preferred_element_type=F32)
    x1_ref[...] = x1
    h2 = _rms(x1, ng_ref[...])
    h2_ref[...] = h2

    logits = jnp.dot(h2, wr_ref[...], preferred_element_type=F32,
                     precision=lax.Precision.HIGHEST) + br_ref[...]
    lane = lax.broadcasted_iota(jnp.int32, logits.shape, 1)
    neg = jnp.float32(-jnp.inf)
    work = jnp.where(lane < N_EXPERTS, logits, neg)
    top_i = jnp.zeros(logits.shape, jnp.int32)
    top_e = jnp.zeros(logits.shape, F32)
    v0 = None
    picks = []
    for kk in range(TOP_K):
        m = jnp.max(work, axis=-1, keepdims=True)
        idx = jnp.min(jnp.where(work == m, lane, LANES), axis=-1, keepdims=True)
        if kk == 0:
            v0 = m
        pick = lane == idx
        picks.append(pick)
        top_i = jnp.where(lane == kk, idx, top_i)
        top_e = jnp.where(lane == kk, jnp.exp(m - v0), top_e)
        work = jnp.where(pick, neg, work)
    ti_ref[...] = top_i
    tw_ref[...] = top_e / jnp.sum(top_e, axis=-1, keepdims=True)

    tm = logits.shape[0]
    sel = sum(p.astype(F32) for p in picks)
    earlier = (lax.broadcasted_iota(jnp.int32, (tm, tm), 0) > lax.broadcasted_iota(jnp.int32, (tm, tm), 1))
    before = cnt_ref[...] + jnp.dot(earlier.astype(BF16), sel.astype(BF16), preferred_element_type=F32)
    rank = jnp.zeros(logits.shape, jnp.int32)
    for kk in range(TOP_K):
        r_k = jnp.sum(jnp.where(picks[kk], before, 0.0), axis=-1, keepdims=True).astype(jnp.int32)
        rank = jnp.where(lane == kk, r_k, rank)
    rk_ref[...] = rank
    cnt_ref[...] = cnt_ref[...] + jnp.sum(sel, axis=0, keepdims=True)


def _proj(ya_in, yb_in, proj, x2d, w_br_a, w_br_b, w_out, norm_g, w_router, b_router, *, tm=256):
    n = x2d.shape[0]

    def rows(width, cblk=0):
        return pl.BlockSpec((tm, width), lambda i: (i, cblk))

    def whole(shape):
        return pl.BlockSpec(shape, lambda i: (0,) * len(shape), pipeline_mode=pl.Buffered(1))

    return pl.pallas_call(
        _proj_kernel,
        grid=(n // tm,),
        in_specs=[
            rows(CONV_WIDTH), rows(GLA_HEADS * GLA_DV),
            rows(D_MODEL, COL_GA // D_MODEL), rows(D_MODEL, COL_GB // D_MODEL),
            rows(D_MODEL),
            whole((CONV_WIDTH, D_MODEL)), whole((GLA_HEADS * GLA_DV, D_MODEL)), whole((D_MODEL, D_MODEL)),
            whole((1, D_MODEL)), whole((D_MODEL, LANES)), whole((1, LANES)),
        ],
        out_specs=[rows(D_MODEL), rows(D_MODEL), rows(LANES), rows(LANES), rows(LANES),
                   pl.BlockSpec((1, LANES), lambda i: (0, 0))],
        out_shape=[
            jax.ShapeDtypeStruct((n, D_MODEL), F32),
            jax.ShapeDtypeStruct((n, D_MODEL), F32),
            jax.ShapeDtypeStruct((n, LANES), jnp.int32),
            jax.ShapeDtypeStruct((n, LANES), F32),
            jax.ShapeDtypeStruct((n, LANES), jnp.int32),
            jax.ShapeDtypeStruct((1, LANES), F32),
        ],
        compiler_params=pltpu.CompilerParams(
            dimension_semantics=("arbitrary",), vmem_limit_bytes=56 * MIB),
        name="proj",
    )(ya_in, yb_in, proj, proj, x2d, w_br_a, w_br_b, w_out, norm_g, w_router, b_router)


def _lane_sum(x):
    return jnp.sum(x, axis=-1, keepdims=True)


def _route_kernel(ti_ref, rk_ref, cnt_ref, dest_ref, te_ref, meta_ref, *, log_tm, n_te):
    tm = 1 << log_tm
    lane8 = lax.broadcasted_iota(jnp.int32, (HALO, LANES), 1)
    cnt = jnp.broadcast_to(cnt_ref[...], (HALO, LANES)).astype(jnp.int32)
    padded = ((cnt + (tm - 1)) >> log_tm) << log_tm
    pad_end = padded
    for s in (1, 2, 4, 8, 16):
        pad_end = pad_end + jnp.where(lane8 >= s, pltpu.roll(pad_end, s, axis=1), 0)
    pad_start = pad_end - padded

    @pl.when(pl.program_id(0) == 0)
    def _():
        tile_start = lax.broadcasted_iota(jnp.int32, (n_te, LANES), 0) * tm
        lane_t = lax.broadcasted_iota(jnp.int32, (n_te, LANES), 1)
        done = (pad_end[0:1, :] <= tile_start) & (lane_t < N_EXPERTS)
        tile_e = jnp.minimum(_lane_sum(done.astype(F32)), N_EXPERTS - 1.0)
        te_ref[...] = jnp.broadcast_to(tile_e, (n_te, LANES)).astype(jnp.int32)
        total = _lane_sum(jnp.where(lane8 == N_EXPERTS - 1, pad_end, 0).astype(F32)).astype(jnp.int32)
        row8 = lax.broadcasted_iota(jnp.int32, (HALO, LANES), 0)
        meta_ref[...] = jnp.where(row8 == 0, total >> log_tm, jnp.where(row8 == 1, pad_end, cnt))

    lane = lax.broadcasted_iota(jnp.int32, ti_ref.shape, 1)
    top_f = ti_ref[...].astype(F32)
    start_f = pad_start[0:1, :].astype(F32)
    dest = rk_ref[...]
    for kk in range(TOP_K):
        e_k = _lane_sum(jnp.where(lane == kk, top_f, 0.0)).astype(jnp.int32)
        start_k = _lane_sum(jnp.where(lane == e_k, start_f, 0.0)).astype(jnp.int32)
        dest = dest + jnp.where(lane == kk, start_k, 0)
    dest_ref[...] = dest


def _route(top_i, rank, counts, *, log_tm, n_te, tr=2048):
    n = top_i.shape[0]
    return pl.pallas_call(
        functools.partial(_route_kernel, log_tm=log_tm, n_te=n_te),
        grid=(n // tr,),
        in_specs=[
            pl.BlockSpec((tr, LANES), lambda i: (i, 0)),
            pl.BlockSpec((tr, LANES), lambda i: (i, 0)),
            pl.BlockSpec((1, LANES), lambda i: (0, 0)),
        ],
        out_specs=[
            pl.BlockSpec((tr, LANES), lambda i: (i, 0)),
            pl.BlockSpec((n_te, LANES), lambda i: (0, 0)),
            pl.BlockSpec((HALO, LANES), lambda i: (0, 0)),
        ],
        out_shape=[
            jax.ShapeDtypeStruct((n, LANES), jnp.int32),
            jax.ShapeDtypeStruct((n_te, LANES), jnp.int32),
            jax.ShapeDtypeStruct((HALO, LANES), jnp.int32),
        ],
        compiler_params=pltpu.CompilerParams(dimension_semantics=("arbitrary",)),
        name="route",
    )(top_i, rank, counts)


def _scatter_kernel(pe_ref, cnt_ref, nu_ref, dest_ref, hp_ref, xs_ref, zero_ref, sem, *, tc, tm, n_tiles):
    @pl.when(pl.program_id(0) == 0)
    def _():
        zero_ref[...] = jnp.zeros_like(zero_ref)

        def last_tile(e):
            return pltpu.make_async_copy(
                zero_ref, xs_ref.at[pl.ds(pl.multiple_of(pe_ref[e] - tm, tm), tm)], sem.at[0])

        def tail_tile(j):
            return pltpu.make_async_copy(zero_ref, xs_ref.at[pl.ds(pl.multiple_of(j * tm, tm), tm)], sem.at[0])

        def start_tail(j, carry):
            tail_tile(j).start()
            return carry

        def wait_tail(j, carry):
            tail_tile(j).wait()
            return carry

        for e in range(N_EXPERTS):
            @pl.when(cnt_ref[e] > 0)
            def _():
                last_tile(e).start()
        lax.fori_loop(nu_ref[0], n_tiles, start_tail, 0)
        for e in range(N_EXPERTS):
            @pl.when(cnt_ref[e] > 0)
            def _():
                last_tile(e).wait()
        lax.fori_loop(nu_ref[0], n_tiles, wait_tail, 0)

    def issue(t, carry):
        for kk in range(TOP_K):
            pltpu.make_async_copy(hp_ref.at[pl.ds(t, 1)],
                                  xs_ref.at[pl.ds(dest_ref[t * TOP_K + kk], 1)], sem.at[1]).start()
        return carry

    lax.fori_loop(0, tc, issue, 0, unroll=2)
    for kk in range(TOP_K):
        pltpu.make_async_copy(hp_ref, xs_ref.at[pl.ds(0, tc)], sem.at[1]).wait()


def _scatter(pad_end, cnt, n_used, dest_flat, hp, *, n_slots, tm, tc=256):
    n = hp.shape[0]
    return pl.pallas_call(
        functools.partial(_scatter_kernel, tc=tc, tm=tm, n_tiles=n_slots // tm),
        grid_spec=pltpu.PrefetchScalarGridSpec(
            num_scalar_prefetch=3,
            grid=(n // tc,),
            in_specs=[
                pl.BlockSpec((TOP_K * tc,), lambda i, pe, ct, nu: (i,), memory_space=pltpu.SMEM),
                pl.BlockSpec((tc, D_MODEL), lambda i, pe, ct, nu: (i, 0)),
            ],
            out_specs=pl.BlockSpec(memory_space=pl.ANY),
            scratch_shapes=[pltpu.VMEM((tm, D_MODEL), F32), pltpu.SemaphoreType.DMA((2,))],
        ),
        out_shape=jax.ShapeDtypeStruct((n_slots, D_MODEL), F32),
        compiler_params=pltpu.CompilerParams(dimension_semantics=("arbitrary",), vmem_limit_bytes=32 * MIB),
        name="scatter",
    )(pad_end, cnt, n_used, dest_flat, hp)


def _experts_kernel(te_ref, nu_ref, x_ref, wg_ref, wu_ref, wd_ref, bg_ref, bu_ref, bd_ref,
                    out_ref, xb_ref, wgu_ref, *, n_f):
    i = pl.program_id(0)
    f = pl.program_id(1)

    @pl.when((i >= nu_ref[0]) & (f == n_f - 1))
    def _():
        out_ref[...] = jnp.zeros_like(out_ref)

    @pl.when(i < nu_ref[0])
    def _():
        @pl.when(f == 0)
        def _():
            xb_ref[...] = x_ref[...].astype(BF16)

        tf = wg_ref.shape[1]
        wgu_ref[:, :tf] = wg_ref[...].astype(BF16)
        wgu_ref[:, tf:] = wu_ref[...].astype(BF16)
        gu = jnp.dot(xb_ref[...], wgu_ref[...], preferred_element_type=F32)
        gate = gu[:, :tf] + bg_ref[...]
        up = gu[:, tf:] + bu_ref[...]
        gate = jnp.minimum(gate, SWIGLU_LIMIT)
        up = jnp.clip(up, -SWIGLU_LIMIT, SWIGLU_LIMIT)
        act = (up + 1.0) * (gate * jax.nn.sigmoid(SWIGLU_ALPHA * gate))
        act = act.astype(BF16)

        @pl.when(f == 0)
        def _():
            out_ref[...] = jnp.broadcast_to(bd_ref[...], out_ref.shape)

        for c in range(D_MODEL // DOWN_CHUNK):
            cols = slice(c * DOWN_CHUNK, (c + 1) * DOWN_CHUNK)
            out_ref[:, cols] += jnp.dot(act, wd_ref[:, cols].astype(BF16), preferred_element_type=F32)


def _experts(tile_e, n_used, xs, w_gate_up, w_down, b_gate_up, b_down, *, tm, tf=256):
    n_slots = xs.shape[0]
    n_f = D_FF // tf

    def tile(i, nu):
        return jnp.maximum(jnp.minimum(i, nu[0] - 1), 0)

    def fblk(i, f, nu):
        return jnp.where(i < nu[0], f, n_f - 1)

    return pl.pallas_call(
        functools.partial(_experts_kernel, n_f=n_f),
        grid_spec=pltpu.PrefetchScalarGridSpec(
            num_scalar_prefetch=2,
            grid=(n_slots // tm, n_f),
            in_specs=[
                pl.BlockSpec((tm, D_MODEL), lambda i, f, te, nu: (tile(i, nu), 0), pipeline_mode=pl.Buffered(1)),
                pl.BlockSpec((None, D_MODEL, tf), lambda i, f, te, nu: (te[tile(i, nu)], 0, fblk(i, f, nu))),
                pl.BlockSpec((None, D_MODEL, tf),
                             lambda i, f, te, nu: (te[tile(i, nu)], 0, n_f + fblk(i, f, nu))),
                pl.BlockSpec((None, tf, D_MODEL), lambda i, f, te, nu: (te[tile(i, nu)], fblk(i, f, nu), 0)),
                pl.BlockSpec((None, 1, tf), lambda i, f, te, nu: (te[tile(i, nu)], 0, fblk(i, f, nu))),
                pl.BlockSpec((None, 1, tf), lambda i, f, te, nu: (te[tile(i, nu)], 0, n_f + fblk(i, f, nu))),
                pl.BlockSpec((None, 1, D_MODEL), lambda i, f, te, nu: (te[tile(i, nu)], 0, 0)),
            ],
            out_specs=pl.BlockSpec((tm, D_MODEL), lambda i, f, te, nu: (i, 0)),
            scratch_shapes=[pltpu.VMEM((tm, D_MODEL), BF16), pltpu.VMEM((D_MODEL, 2 * tf), BF16)],
        ),
        out_shape=jax.ShapeDtypeStruct((n_slots, D_MODEL), F32),
        compiler_params=pltpu.CompilerParams(
            dimension_semantics=("arbitrary", "arbitrary"), vmem_limit_bytes=56 * MIB),
        name="experts",
    )(tile_e, n_used, xs, w_gate_up, w_gate_up, w_down, b_gate_up, b_gate_up, b_down)


def _combine_kernel(dest_ref, dnext_ref, tw_ref, x1_ref, g_ref, yb_ref, out_ref, buf_ref, sem, *, tc, n_tiles):
    i = pl.program_id(0)
    slot = i % 2

    def issue(idx_ref, s):
        def body(t, carry):
            for kk in range(TOP_K):
                pltpu.make_async_copy(yb_ref.at[pl.ds(idx_ref[t * TOP_K + kk], 1)],
                                      buf_ref.at[s, pl.ds(kk * tc + t, 1)], sem.at[s]).start()
            return carry
        lax.fori_loop(0, tc, body, 0, unroll=2)

    @pl.when(i == 0)
    def _():
        issue(dest_ref, 0)

    @pl.when(i + 1 < n_tiles)
    def _():
        issue(dnext_ref, 1 - slot)

    pltpu.make_async_copy(yb_ref.at[pl.ds(0, TOP_K * tc)], buf_ref.at[slot], sem.at[slot]).wait()

    lane = lax.broadcasted_iota(jnp.int32, (tc, LANES), 1)
    tw = tw_ref[...]
    x2 = x1_ref[...]
    for kk in range(TOP_K):
        w = jnp.sum(jnp.where(lane == kk, tw, 0.0), axis=-1, keepdims=True)
        x2 = x2 + w * buf_ref[slot, kk * tc:(kk + 1) * tc, :]
    out_ref[...] = _rms(x2, g_ref[...])


def _combine(dest_flat, top_w, x1, norm_g, yb, *, tc=256):
    n = x1.shape[0]
    n_tiles = n // tc
    return pl.pallas_call(
        functools.partial(_combine_kernel, tc=tc, n_tiles=n_tiles),
        grid=(n_tiles,),
        in_specs=[
            pl.BlockSpec((TOP_K * tc,), lambda i: (i,), memory_space=pltpu.SMEM),
            pl.BlockSpec((TOP_K * tc,), lambda i: (jnp.minimum(i + 1, n_tiles - 1),), memory_space=pltpu.SMEM),
            pl.BlockSpec((tc, LANES), lambda i: (i, 0)),
            pl.BlockSpec((tc, D_MODEL), lambda i: (i, 0)),
            pl.BlockSpec((1, D_MODEL), lambda i: (0, 0)),
            pl.BlockSpec(memory_space=pl.ANY),
        ],
        out_specs=pl.BlockSpec((tc, D_MODEL), lambda i: (i, 0)),
        out_shape=jax.ShapeDtypeStruct((n, D_MODEL), F32),
        scratch_shapes=[pltpu.VMEM((2, TOP_K * tc, D_MODEL), F32), pltpu.SemaphoreType.DMA((2,))],
        compiler_params=pltpu.CompilerParams(
            dimension_semantics=("arbitrary",), vmem_limit_bytes=40 * MIB),
        name="combine",
    )(dest_flat, dest_flat, top_w, x1, norm_g, yb)


def kernel(x, norm_mix_g, w_in, conv_w, conv_b, w_a2, b_a, gla_norm_g, w_br_a, w_br_b, w_out, norm_ffn_g,
           w_router, b_router, w_gate_up, b_gate_up, w_down, b_down, norm_final_g):
    bn, t, d = x.shape
    n = bn * t
    log_tm_e = 10
    tm_e = 1 << log_tm_e
    n_slots = n * TOP_K + N_EXPERTS * tm_e
    n_tiles = n_slots // tm_e
    x2d = x.reshape(n, d)
    assert w_in.shape[0] == 1, "single-layer block: the final norm is fused into the combine step"
    for l in range(1):
        wl = w_in[l]
        w_cat = jnp.concatenate([wl[:, :A_LOW_START], wl[:, A_LOW_START + GLA_LOWRANK:]], axis=1).astype(BF16)
        w_alow = jnp.pad(wl[:, A_LOW_START:A_LOW_START + GLA_LOWRANK],
                         ((0, 0), (0, LANES - GLA_LOWRANK))).astype(BF16)
        proj, a_low = _inproj(x2d, norm_mix_g[l][None, :], w_cat, w_alow)

        w_a2p = jnp.pad(w_a2[l], ((0, LANES - GLA_LOWRANK), (0, 0)))
        ya_in, yb_in = _mixer(proj, a_low, conv_w[l], conv_b[l][None, :], w_a2p, b_a[l][None, :],
                              gla_norm_g[l].reshape(1, -1), batch=bn, seq=t)

        w_rp = jnp.pad(w_router[l], ((0, 0), (0, LANES - N_EXPERTS)))
        b_rp = jnp.pad(b_router[l], (0, LANES - N_EXPERTS))[None, :]
        x1, hp, top_i, top_w, rank, counts = _proj(
            ya_in, yb_in, proj, x2d, w_br_a[l].astype(BF16), w_br_b[l].astype(BF16),
            w_out[l].astype(BF16), norm_ffn_g[l][None, :], w_rp, b_rp)

        dest, te, meta = _route(top_i, rank, counts, log_tm=log_tm_e, n_te=pl.cdiv(n_tiles, HALO) * HALO)
        tile_e = te[:n_tiles, 0]
        n_used = meta[0, :1]
        dest_flat = dest[:, :TOP_K].reshape(-1)
        xs = _scatter(meta[1, :N_EXPERTS], meta[2, :N_EXPERTS], n_used, dest_flat, hp, n_slots=n_slots, tm=tm_e)
        yb = _experts(tile_e, n_used, xs, w_gate_up[l], w_down[l],
                      b_gate_up[l][:, None, :], b_down[l][:, None, :], tm=tm_e)
        x2d = _combine(dest_flat, top_w, x1, norm_final_g[None, :], yb)
    return x2d.reshape(bn, t, d)
```
